```python
import jax, jax.numpy as jnp
from jax import lax
import numpy as np

D_MODEL = 2048
BATCH = 8
SEQ = 4096
DEPTH = 2
DEC_BATCH = 32
DEC_SEQ = 32
PAST_LEN = 2048

CHUNK = 64
CONV_WIDTH = 31
CONV_DIM = D_MODEL
HG_HEADS = 16
HG_KDIM = 128
HG_VDIM = 128
HG_DIM = HG_HEADS * HG_KDIM
HG_VWIDTH = HG_HEADS * HG_VDIM
EPS = 1e-6
IN_COLS = 3 * CONV_DIM + 2 * HG_DIM + 2 * HG_VWIDTH + 2 * D_MODEL

kernel_name = "hybrid_conformer_hgrn2_stream_step"


def _split_points():
    sizes = (CONV_DIM, CONV_DIM, CONV_DIM, HG_DIM, HG_DIM, HG_VWIDTH, HG_VWIDTH, D_MODEL, D_MODEL)
    pts, acc = [], 0
    for s in sizes[:-1]:
        acc += s
        pts.append(acc)
    return pts


def rmsnorm(x, g):
    xf = x.astype(jnp.float32)
    y = xf * lax.rsqrt(jnp.mean(xf * xf, axis=-1, keepdims=True) + EPS)
    return (y * g.astype(jnp.float32)).astype(x.dtype)


def layernorm(x, g, b):
    xf = x.astype(jnp.float32)
    mu = jnp.mean(xf, axis=-1, keepdims=True)
    var = jnp.mean(jnp.square(xf - mu), axis=-1, keepdims=True)
    y = (xf - mu) * lax.rsqrt(var + EPS)
    return (y * g.astype(jnp.float32) + b.astype(jnp.float32)).astype(x.dtype)


def hgrn2_recurrence(q, logf, v, S0):
    B, T, H, K = q.shape
    V = v.shape[-1]
    L = min(CHUNK, T)
    n = T // L
    k = -jnp.expm1(logf)

    def to_blocks(a):
        return a.reshape(B, n, L, H, a.shape[-1]).transpose(1, 0, 3, 2, 4)

    qc, lc, kc, vc = to_blocks(q), to_blocks(logf), to_blocks(k), to_blocks(v)
    causal = jnp.tril(jnp.ones((L, L), dtype=bool))[:, :, None]

    def step(S, inp):
        qb, lfb, kb, vb = inp
        b = jnp.cumsum(lfb, axis=2)
        o_inter = jnp.einsum('bhtk,bhkv->bhtv', qb * jnp.exp(b), S)
        diff = b[:, :, :, None, :] - b[:, :, None, :, :]
        decay = jnp.exp(jnp.where(causal, diff, -jnp.inf))
        scores = jnp.einsum('bhtk,bhtsk,bhsk->bhts', qb, decay, kb)
        o_intra = jnp.einsum('bhts,bhsv->bhtv', scores, vb)
        b_last = b[:, :, -1:, :]
        S_new = jnp.exp(b_last[:, :, 0, :])[..., None] * S + jnp.einsum(
            'bhsk,bhsv->bhkv', kb * jnp.exp(b_last - b), vb)
        return S_new, o_inter + o_intra

    S_fin, o = lax.scan(step, S0, (qc, lc, kc, vc))
    o = o.transpose(1, 0, 3, 2, 4).reshape(B, T, H, V)
    return o, S_fin


def mixer_layer(x, conv_buf, S0, lb, w_in, conv_w, conv_b, conv_ln_g, conv_ln_b,
                hg_norm_g, w_pc, w_ph, w_out, g_pre, g_post):
    B, T, _ = x.shape
    h = rmsnorm(x, g_pre)
    proj = h @ w_in
    ga, gb, zc, q, fa, iv, zh, gcm, ghm = jnp.split(proj, _split_points(), axis=-1)

    u = ga * jax.nn.sigmoid(gb)
    upad = jnp.concatenate([conv_buf.astype(u.dtype), u], axis=1)
    new_buf = upad[:, -(CONV_WIDTH - 1):]
    c = lax.conv_general_dilated(
        upad, conv_w[:, None, :].astype(upad.dtype), window_strides=(1,), padding='VALID',
        dimension_numbers=('NWC', 'WIO', 'NWC'), feature_group_count=CONV_DIM) + conv_b
    c = jax.nn.silu(layernorm(c, conv_ln_g, conv_ln_b)) * jax.nn.silu(zc)
    y_conv = c @ w_pc

    lbh = lb.reshape(HG_HEADS, HG_KDIM)
    fa32 = fa.astype(jnp.float32).reshape(B, T, HG_HEADS, HG_KDIM)
    logf = jnp.logaddexp(jnp.log(lbh), jnp.log1p(-lbh) + jax.nn.log_sigmoid(fa32))
    q32 = q.astype(jnp.float32).reshape(B, T, HG_HEADS, HG_KDIM)
    v32 = iv.astype(jnp.float32).reshape(B, T, HG_HEADS, HG_VDIM)
    o, S_fin = hgrn2_recurrence(q32, logf, v32, S0.astype(jnp.float32))
    o = o * lax.rsqrt(jnp.mean(o * o, axis=-1, keepdims=True) + EPS)
    o = (o * hg_norm_g.astype(jnp.float32).reshape(HG_HEADS, HG_VDIM)).reshape(B, T, HG_VWIDTH)
    y_hg = (o.astype(x.dtype) * jax.nn.silu(zh)) @ w_ph

    m = jax.nn.sigmoid(gcm) * y_conv + jax.nn.sigmoid(ghm) * y_hg
    out = rmsnorm(m @ w_out, g_post)
    return x + out, new_buf, S_fin


def setup_inputs(seed: int = 0) -> dict:
    key = jax.random.key(seed)
    ks = jax.random.split(key, 20)
    f32 = jnp.float32
    nrm = lambda k, shape, s: (jax.random.normal(k, shape, f32) * s).astype(f32)
    return {
        "x_prompt": nrm(ks[0], (BATCH, SEQ, D_MODEL), 1.0),
        "x_sample": nrm(ks[1], (DEC_BATCH, DEC_SEQ, D_MODEL), 1.0),
        "cache_conv": nrm(ks[2], (DEPTH, DEC_BATCH, CONV_WIDTH - 1, CONV_DIM), 0.5),
        "state_hgrn": nrm(ks[3], (DEPTH, DEC_BATCH, HG_HEADS, HG_KDIM, HG_VDIM), 0.3),
        "lb_logits": nrm(ks[4], (DEPTH, HG_DIM), 0.5),
        "w_in": nrm(ks[5], (DEPTH, D_MODEL, IN_COLS), D_MODEL ** -0.5),
        "conv_w": nrm(ks[6], (DEPTH, CONV_WIDTH, CONV_DIM), CONV_WIDTH ** -0.5),
        "conv_b": nrm(ks[7], (DEPTH, CONV_DIM), 0.02),
        "conv_ln_g": 1.0 + nrm(ks[8], (DEPTH, CONV_DIM), 0.02),
        "conv_ln_b": nrm(ks[9], (DEPTH, CONV_DIM), 0.02),
        "hg_norm_g": 1.0 + nrm(ks[10], (DEPTH, HG_VWIDTH), 0.02),
        "w_pc": nrm(ks[11], (DEPTH, CONV_DIM, D_MODEL), CONV_DIM ** -0.5),
        "w_ph": nrm(ks[12], (DEPTH, HG_VWIDTH, D_MODEL), HG_VWIDTH ** -0.5),
        "w_out": nrm(ks[13], (DEPTH, D_MODEL, D_MODEL), D_MODEL ** -0.5),
        "norm_pre_g": 1.0 + nrm(ks[14], (DEPTH, D_MODEL), 0.02),
        "norm_post_g": 1.0 + nrm(ks[15], (DEPTH, D_MODEL), 0.02),
    }


def reference(x_prompt, x_sample, cache_conv, state_hgrn, lb_logits, w_in, conv_w, conv_b,
              conv_ln_g, conv_ln_b, hg_norm_g, w_pc, w_ph, w_out, norm_pre_g, norm_post_g):
    lb_all = jnp.cumsum(jax.nn.softmax(lb_logits.astype(jnp.float32), axis=0), axis=0)
    lb_all = lb_all - lb_all[0:1]
    bp = x_prompt.shape[0]
    buf0 = jnp.zeros((bp, CONV_WIDTH - 1, CONV_DIM), x_prompt.dtype)
    s0 = jnp.zeros((bp, HG_HEADS, HG_KDIM, HG_VDIM), jnp.float32)
    hp, hs = x_prompt, x_sample
    conv_p, hg_p, conv_s, hg_s = [], [], [], []
    for l in range(DEPTH):
        params = (lb_all[l], w_in[l], conv_w[l], conv_b[l], conv_ln_g[l], conv_ln_b[l],
                  hg_norm_g[l], w_pc[l], w_ph[l], w_out[l], norm_pre_g[l], norm_post_g[l])
        hp, cbp, sfp = mixer_layer(hp, buf0, s0, *params)
        hs, cbs, sfs = mixer_layer(hs, cache_conv[l], state_hgrn[l], *params)
        conv_p.append(cbp); hg_p.append(sfp); conv_s.append(cbs); hg_s.append(sfs)
    new_conv_prompt = jnp.stack(conv_p)
    new_hgrn_prompt = jnp.stack(hg_p)
    new_conv_sample = jnp.stack(conv_s)
    new_hgrn_sample = jnp.stack(hg_s)
    return (hp, hs, new_conv_prompt, new_hgrn_prompt, new_conv_sample, new_hgrn_sample)
```

```python
import functools
import math

import numpy as np
import jax
import jax.numpy as jnp
from jax import lax
from jax.experimental import pallas as pl
from jax.experimental.pallas import tpu as pltpu

EPS = 1e-6
HEAD_DIM = 128
CONV_WIDTH = 31
HIST_ROWS = 32
NUM_GROUPS = 9
SLAB = 256
HGRN_CHUNK = 64
V7X_VMEM_LIMIT = 56 * 1024 * 1024

F32 = jnp.float32
BF16 = jnp.bfloat16
_NT = (((1,), (1,)), ((), ()))
_TN = (((0,), (0,)), ((), ()))


def _params(n_axes):
    return pltpu.CompilerParams(
        dimension_semantics=("arbitrary",) * n_axes,
        vmem_limit_bytes=V7X_VMEM_LIMIT,
    )


def _rms(x, g):
    ms = jnp.mean(x * x, axis=-1, keepdims=True)
    return x * lax.rsqrt(ms + EPS) * g


def _prenorm_kernel(n_prompt_blocks, xp_ref, xs_ref, g_ref, h_ref):
    i = pl.program_id(0)

    def body(x_ref):
        h_ref[...] = _rms(x_ref[...], g_ref[...]).astype(BF16)

    pl.when(i < n_prompt_blocks)(lambda: body(xp_ref))
    pl.when(i >= n_prompt_blocks)(lambda: body(xs_ref))


def _prenorm(xp, xs, g, tm):
    n_p, d = xp.shape
    n_s = xs.shape[0]
    npb, nsb = n_p // tm, n_s // tm
    return pl.pallas_call(
        functools.partial(_prenorm_kernel, npb),
        grid=(npb + nsb,),
        in_specs=[
            pl.BlockSpec((tm, d), lambda i: (jnp.minimum(i, npb - 1), 0)),
            pl.BlockSpec((tm, d), lambda i: (jnp.maximum(i - npb, 0), 0)),
            pl.BlockSpec((1, d), lambda i: (0, 0)),
        ],
        out_specs=pl.BlockSpec((tm, d), lambda i: (i, 0)),
        out_shape=jax.ShapeDtypeStruct((n_p + n_s, d), BF16),
        compiler_params=_params(1),
        name="prenorm",
    )(xp, xs, g)


def _inproj_kernel(h_ref, w_ref, lb_ref, u_ref, szc_ref, q_ref, lf_ref, k_ref,
                   v_ref, szh_ref, sgc_ref, sgh_ref):
    h = h_ref[...]

    def mm(g):
        return jnp.dot(h, w_ref[0, :, g * SLAB:(g + 1) * SLAB],
                       preferred_element_type=F32)

    u_ref[...] = (mm(0) * jax.nn.sigmoid(mm(1))).astype(BF16)
    zc = mm(2)
    szc_ref[...] = (zc * jax.nn.sigmoid(zc)).astype(BF16)
    q_ref[...] = mm(3).astype(BF16)

    fa = mm(4)
    lb = lb_ref[...]
    softplus_neg = jnp.log1p(jnp.exp(-jnp.abs(fa)))
    log_sig = jnp.minimum(fa, 0.0) - softplus_neg
    log_sig_neg = -jnp.maximum(fa, 0.0) - softplus_neg
    la = jnp.log(lb)
    y = jnp.log1p(-lb) + log_sig
    lf_ref[...] = jnp.maximum(la, y) + jnp.log1p(jnp.exp(-jnp.abs(la - y)))
    k_ref[...] = ((1.0 - lb) * jnp.exp(log_sig_neg)).astype(BF16)

    v_ref[...] = mm(5).astype(BF16)
    zh = mm(6)
    szh_ref[...] = (zh * jax.nn.sigmoid(zh)).astype(BF16)
    sgc_ref[...] = jax.nn.sigmoid(mm(7)).astype(BF16)
    sgh_ref[...] = jax.nn.sigmoid(mm(8)).astype(BF16)


def _inproj(h, w_slabs, lb, tm):
    n, d = h.shape
    n_slabs = w_slabs.shape[0]
    out_spec = pl.BlockSpec((tm, SLAB), lambda j, i: (i, j))
    bf = jax.ShapeDtypeStruct((n, d), BF16)
    f32 = jax.ShapeDtypeStruct((n, d), F32)
    return pl.pallas_call(
        _inproj_kernel,
        grid=(n_slabs, n // tm),
        in_specs=[
            pl.BlockSpec((tm, d), lambda j, i: (i, 0)),
            pl.BlockSpec((1, d, NUM_GROUPS * SLAB), lambda j, i: (j, 0, 0)),
            pl.BlockSpec((1, SLAB), lambda j, i: (0, j)),
        ],
        out_specs=[out_spec] * 9,
        out_shape=[bf, bf, bf, f32, bf, bf, bf, bf, bf],
        compiler_params=_params(2),
        name="inproj",
    )(h, w_slabs, lb)


def _conv_kernel(n_seg, seg, rb, has_hist, *refs):
    if has_hist:
        (u_ref, szc_ref, sgc_ref, hist_ref, cw_ref, cb_ref, lng_ref, lnb_ref, wpc_ref,
         mc_ref, ubuf, cbuf) = refs
    else:
        (u_ref, szc_ref, sgc_ref, cw_ref, cb_ref, lng_ref, lnb_ref, wpc_ref,
         mc_ref, ubuf, cbuf) = refs
        hist_ref = None
    d = u_ref.shape[-1]
    t = pl.program_id(1)
    stride = seg + HIST_ROWS

    @pl.when(t == 0)
    def _():
        for s in range(n_seg):
            if has_hist:
                ubuf[s * stride:s * stride + HIST_ROWS, :] = hist_ref[s]
            else:
                ubuf[s * stride:s * stride + HIST_ROWS, :] = jnp.zeros((HIST_ROWS, d), F32)

    for s in range(n_seg):
        ubuf[s * stride + HIST_ROWS:(s + 1) * stride, :] = (
            u_ref[s * seg:(s + 1) * seg, :].astype(F32))

    first = HIST_ROWS - (CONV_WIDTH - 1)

    def col_body(ci, carry):
        cols = pl.ds(pl.multiple_of(ci * 128, 128), 128)
        for s in range(n_seg):
            for r0 in range(0, seg, rb):
                base = s * stride + r0 + first
                acc = jnp.zeros((rb, 128), F32)
                for j in range(CONV_WIDTH):
                    acc = acc + cw_ref[j:j + 1, cols] * ubuf[base + j:base + j + rb, cols]
                cbuf[s * seg + r0:s * seg + r0 + rb, cols] = acc + cb_ref[:, cols]
        return carry

    lax.fori_loop(0, d // 128, col_body, 0)

    c = cbuf[...]
    mu = jnp.mean(c, axis=-1, keepdims=True)
    cc = c - mu
    var = jnp.mean(cc * cc, axis=-1, keepdims=True)
    y = cc * lax.rsqrt(var + EPS) * lng_ref[...] + lnb_ref[...]
    a = (y * jax.nn.sigmoid(y) * szc_ref[...].astype(F32)).astype(BF16)
    yc = jnp.dot(a, wpc_ref[...], preferred_element_type=F32)
    mc_ref[...] = (sgc_ref[...].astype(F32) * yc).astype(BF16)

    if not has_hist:
        ubuf[0:HIST_ROWS, :] = ubuf[seg:seg + HIST_ROWS, :]


def _conv_branch(u, szc, sgc, hist, cw, cb, lng, lnb, wpc, *, n_seq, seq_len, row0, name):
    d = u.shape[-1]
    if hist is None:
        n_seg, seg = 1, min(seq_len, 256)
        nt = seq_len // seg
        grid = (n_seq, nt)
    else:
        seg = seq_len
        n_seg = math.gcd(n_seq, max(1, 256 // seg))
        nt = 1
        grid = (n_seq // n_seg, 1)
    tt = n_seg * seg
    rb = min(seg, 64)
    rb0 = row0 // tt
    tok = pl.BlockSpec((tt, d), lambda b, t: (rb0 + b * nt + t, 0))
    vec = pl.BlockSpec((1, d), lambda b, t: (0, 0))
    in_specs = [tok, tok, tok]
    args = [u, szc, sgc]
    if hist is not None:
        in_specs.append(pl.BlockSpec((n_seg, HIST_ROWS, d), lambda b, t: (b, 0, 0)))
        args.append(hist)
    in_specs += [pl.BlockSpec((CONV_WIDTH, d), lambda b, t: (0, 0)), vec, vec, vec,
                 pl.BlockSpec((d, d), lambda b, t: (0, 0), pipeline_mode=pl.Buffered(1))]
    args += [cw, cb, lng, lnb, wpc]
    return pl.pallas_call(
        functools.partial(_conv_kernel, n_seg, seg, rb, hist is not None),
        grid=grid,
        in_specs=in_specs,
        out_specs=pl.BlockSpec((tt, d), lambda b, t: (b * nt + t, 0)),
        out_shape=jax.ShapeDtypeStruct((n_seq * seq_len, d), BF16),
        scratch_shapes=[pltpu.VMEM((n_seg * (seg + HIST_ROWS), d), F32),
                        pltpu.VMEM((tt, d), F32)],
        compiler_params=_params(2),
        name=name,
    )(*args)


def _hgrn_tables(blk):
    n_levels = int(math.log2(blk))
    r = np.arange(blk)[:, None]
    j = np.arange(blk)[None, :]
    mats = [j <= r, j > r]
    masks = [r == j]
    for lvl in range(n_levels):
        c = blk >> (lvl + 1)
        second = ((r // c) % 2) == 1
        ref = (r // (2 * c)) * 2 * c + c - 1
        mats.append(np.where(second, (j > ref) & (j <= r), (j > r) & (j <= ref)))
        first_s = ((j // c) % 2) == 0
        masks.append(((r // (2 * c)) == (j // (2 * c))) & second & first_s)
    m_all = np.concatenate(mats, axis=0).astype(np.float32)
    kpad = -(-3 * blk // 128) * 128
    m3 = np.zeros((m_all.shape[0], kpad), np.float32)
    for s in range(3):
        m3[:, s * blk:(s + 1) * blk] = m_all
    return n_levels, m3, np.stack(masks).astype(np.float32)


def _hgrn_kernel(blk, n_blocks, n_levels, has_s0, *refs):
    if has_s0:
        (q_ref, k_ref, v_ref, lf_ref, szh_ref, g_ref, m3_ref, mask_ref, s0_ref,
         oz_ref, sfin_ref, st_ref) = refs
    else:
        (q_ref, k_ref, v_ref, lf_ref, szh_ref, g_ref, m3_ref, mask_ref,
         oz_ref, sfin_ref, st_ref) = refs
        s0_ref = None
    n_heads = st_ref.shape[0]
    gw = n_heads * HEAD_DIM
    kpad = m3_ref.shape[1]
    t = pl.program_id(2)

    @pl.when(t == 0)
    def _():
        for hh in range(n_heads):
            if has_s0:
                st_ref[hh] = s0_ref[0, hh].T
            else:
                st_ref[hh] = jnp.zeros((HEAD_DIM, HEAD_DIM), F32)

    def block(bi, carry):
        rows = pl.ds(pl.multiple_of(bi * blk, blk), blk)
        lf = lf_ref[rows, :]
        hi = lf.astype(BF16)
        r1 = lf - hi.astype(F32)
        mid = r1.astype(BF16)
        lo = (r1 - mid.astype(F32)).astype(BF16)
        parts = [hi, mid, lo]
        if kpad > 3 * blk:
            parts.append(jnp.zeros((kpad - 3 * blk, gw), BF16))
        lf3 = jnp.concatenate(parts, axis=0)
        e32 = jnp.exp(jnp.dot(m3_ref[...], lf3, preferred_element_type=F32))
        e16 = e32.astype(BF16)
        for hh in range(n_heads):
            ls = slice(hh * HEAD_DIM, (hh + 1) * HEAD_DIM)
            q = q_ref[rows, ls]
            k = k_ref[rows, ls]
            v = v_ref[rows, ls]
            st = st_ref[hh]
            o = lax.dot_general(q * e16[0:blk, ls], st.astype(BF16), _NT,
                                preferred_element_type=F32)
            sc = lax.dot_general(q, k, _NT, preferred_element_type=F32) * mask_ref[0]
            for lvl in range(n_levels):
                ec = e16[(2 + lvl) * blk:(3 + lvl) * blk, ls]
                sc = sc + lax.dot_general(q * ec, k * ec, _NT,
                                          preferred_element_type=F32) * mask_ref[1 + lvl]
            o = o + jnp.dot(sc.astype(BF16), v, preferred_element_type=F32)
            ku = k * e16[blk:2 * blk, ls]
            st_ref[hh] = st * e32[blk - 1:blk, ls] + lax.dot_general(
                v, ku, _TN, preferred_element_type=F32)
            on = _rms(o, g_ref[:, ls])
            oz_ref[rows, ls] = (on * szh_ref[rows, ls].astype(F32)).astype(BF16)
        return carry

    lax.fori_loop(0, n_blocks, block, 0)

    @pl.when(t == pl.num_programs(2) - 1)
    def _():
        for hh in range(n_heads):
            sfin_ref[0, hh] = st_ref[hh].T


def _hgrn_branch(q, k, v, lf, szh, g, s0, *, n_seq, seq_len, row0, name):
    d = q.shape[-1]
    n_heads_total = d // HEAD_DIM
    gw = SLAB
    hpg = gw // HEAD_DIM
    blk = min(HGRN_CHUNK, seq_len)
    tt = min(seq_len, 256)
    nt = seq_len // tt
    n_levels, m3, masks = _hgrn_tables(blk)
    rb0 = row0 // tt
    tok = pl.BlockSpec((tt, gw), lambda b, g_, t: (rb0 + b * nt + t, g_))
    state = pl.BlockSpec((1, hpg, HEAD_DIM, HEAD_DIM), lambda b, g_, t: (b, g_, 0, 0))
    in_specs = [tok, tok, tok, tok, tok,
                pl.BlockSpec((1, gw), lambda b, g_, t: (0, g_)),
                pl.BlockSpec(m3.shape, lambda b, g_, t: (0, 0)),
                pl.BlockSpec(masks.shape, lambda b, g_, t: (0, 0, 0))]
    args = [q, k, v, lf, szh, g, jnp.asarray(m3, BF16), jnp.asarray(masks, F32)]
    if s0 is not None:
        in_specs.append(state)
        args.append(s0)
    return pl.pallas_call(
        functools.partial(_hgrn_kernel, blk, tt // blk, n_levels, s0 is not None),
        grid=(n_seq, d // gw, nt),
        in_specs=in_specs,
        out_specs=[pl.BlockSpec((tt, gw), lambda b, g_, t: (b * nt + t, g_)), state],
        out_shape=[jax.ShapeDtypeStruct((n_seq * seq_len, d), BF16),
                   jax.ShapeDtypeStruct((n_seq, n_heads_total, HEAD_DIM, HEAD_DIM), F32)],
        scratch_shapes=[pltpu.VMEM((hpg, HEAD_DIM, HEAD_DIM), F32)],
        compiler_params=_params(3),
        name=name,
    )(*args)


def _out_kernel(n_prompt_blocks, emit_h, *refs):
    (mcp_ref, mcs_ref, ozp_ref, ozs_ref, sgh_ref, xp_ref, xs_ref,
     wph_ref, wout_ref, gpost_ref, gnext_ref) = refs[:11]
    if emit_h:
        yp_ref, ys_ref, hn_ref = refs[11:]
    else:
        yp_ref, ys_ref = refs[11:]
        hn_ref = None
    i = pl.program_id(0)

    def body(mc_ref, oz_ref, x_ref, y_ref):
        yh = jnp.dot(oz_ref[...], wph_ref[...], preferred_element_type=F32)
        m = mc_ref[...].astype(F32) + sgh_ref[...].astype(F32) * yh
        o = jnp.dot(m.astype(BF16), wout_ref[...], preferred_element_type=F32)
        y = x_ref[...] + _rms(o, gpost_ref[...])
        y_ref[...] = y
        if emit_h:
            hn_ref[...] = _rms(y, gnext_ref[...]).astype(BF16)

    pl.when(i < n_prompt_blocks)(lambda: body(mcp_ref, ozp_ref, xp_ref, yp_ref))
    pl.when(i >= n_prompt_blocks)(lambda: body(mcs_ref, ozs_ref, xs_ref, ys_ref))


def _out_proj(mc_p, mc_s, oz_p, oz_s, sgh, xp, xs, wph, wout, gpost, gnext,
              *, tm, emit_h):
    n_p, d = xp.shape
    n_s = xs.shape[0]
    npb, nsb = n_p // tm, n_s // tm
    pr = pl.BlockSpec((tm, d), lambda i: (jnp.minimum(i, npb - 1), 0))
    sa = pl.BlockSpec((tm, d), lambda i: (jnp.maximum(i - npb, 0), 0))
    al = pl.BlockSpec((tm, d), lambda i: (i, 0))
    wt = pl.BlockSpec((d, d), lambda i: (0, 0), pipeline_mode=pl.Buffered(1))
    vec = pl.BlockSpec((1, d), lambda i: (0, 0))
    out_specs = [pr, sa]
    out_shape = [jax.ShapeDtypeStruct((n_p, d), F32), jax.ShapeDtypeStruct((n_s, d), F32)]
    if emit_h:
        out_specs.append(al)
        out_shape.append(jax.ShapeDtypeStruct((n_p + n_s, d), BF16))
    return pl.pallas_call(
        functools.partial(_out_kernel, npb, emit_h),
        grid=(npb + nsb,),
        in_specs=[pr, sa, pr, sa, al, pr, sa, wt, wt, vec, vec],
        out_specs=out_specs,
        out_shape=out_shape,
        compiler_params=_params(1),
        name="outproj",
    )(mc_p, mc_s, oz_p, oz_s, sgh, xp, xs, wph, wout, gpost, gnext)


def kernel(x_prompt, x_sample, cache_conv, state_hgrn, lb_logits, w_in, conv_w, conv_b,
           conv_ln_g, conv_ln_b, hg_norm_g, w_pc, w_ph, w_out, norm_pre_g, norm_post_g):
    bp, tp, d = x_prompt.shape
    bs, ts, _ = x_sample.shape
    depth = w_in.shape[0]
    n_p, n_s = bp * tp, bs * ts
    n_slabs = d // SLAB
    assert w_in.shape[-1] == NUM_GROUPS * d and d % SLAB == 0
    assert ts >= CONV_WIDTH - 1 and ts <= HGRN_CHUNK and tp % 256 == 0
    tm_in = math.gcd(512, math.gcd(n_p, n_s))
    tm_out = math.gcd(256, math.gcd(n_p, n_s))

    lb_all = jnp.cumsum(jax.nn.softmax(lb_logits.astype(F32), axis=0), axis=0)
    lb_all = lb_all - lb_all[0:1]

    xp = x_prompt.reshape(n_p, d)
    xs = x_sample.reshape(n_s, d)
    hist = jnp.pad(cache_conv, ((0, 0), (0, 0), (HIST_ROWS - (CONV_WIDTH - 1), 0), (0, 0)))

    row = lambda a, l: a[l].reshape(1, d)
    h = _prenorm(xp, xs, row(norm_pre_g, 0), tm_in)
    conv_p, hg_p, conv_s, hg_s = [], [], [], []
    for l in range(depth):
        w_slabs = (w_in[l].reshape(d, NUM_GROUPS, n_slabs, SLAB).transpose(2, 0, 1, 3)
                   .reshape(n_slabs, d, NUM_GROUPS * SLAB).astype(BF16))
        u, szc, q, lf, k, v, szh, sgc, sgh = _inproj(h, w_slabs, lb_all[l].reshape(1, d), tm_in)

        conv_args = (conv_w[l], row(conv_b, l), row(conv_ln_g, l), row(conv_ln_b, l),
                     w_pc[l].astype(BF16))
        mc_p = _conv_branch(u, szc, sgc, None, *conv_args, n_seq=bp, seq_len=tp, row0=0,
                            name="conv_prompt")
        mc_s = _conv_branch(u, szc, sgc, hist[l], *conv_args, n_seq=bs, seq_len=ts, row0=n_p,
                            name="conv_sample")
        g_h = row(hg_norm_g, l)
        oz_p, sf_p = _hgrn_branch(q, k, v, lf, szh, g_h, None, n_seq=bp, seq_len=tp, row0=0,
                                  name="hgrn_prompt")
        oz_s, sf_s = _hgrn_branch(q, k, v, lf, szh, g_h, state_hgrn[l], n_seq=bs, seq_len=ts,
                                  row0=n_p, name="hgrn_sample")

        last = l == depth - 1
        outs = _out_proj(mc_p, mc_s, oz_p, oz_s, sgh, xp, xs,
                         w_ph[l].astype(BF16), w_out[l].astype(BF16),
                         row(norm_post_g, l), row(norm_pre_g, min(l + 1, depth - 1)),
                         tm=tm_out, emit_h=not last)
        if last:
            xp, xs = outs
        else:
            xp, xs, h = outs

        nb = CONV_WIDTH - 1
        conv_p.append(u[:n_p].reshape(bp, tp, d)[:, tp - nb:].astype(F32))
        conv_s.append(u[n_p:].reshape(bs, ts, d)[:, ts - nb:].astype(F32))
        hg_p.append(sf_p)
        hg_s.append(sf_s)

    return (xp.reshape(bp, tp, d), xs.reshape(bs, ts, d),
            jnp.stack(conv_p), jnp.stack(hg_p), jnp.stack(conv_s), jnp.stack(hg_s))
```

```python
import functools
import math

import numpy as np
import jax
import jax.numpy as jnp
from jax import lax
from jax.experimental import pallas as pl
from jax.experimental.pallas import tpu as pltpu

EPS = 1e-6
HEAD_DIM = 128
CONV_WIDTH = 31
HIST_ROWS = 32
NUM_GROUPS = 9
SLAB = 256
HGRN_CHUNK = 64
V7X_VMEM_LIMIT = 56 * 1024 * 1024

F32 = jnp.float32
BF16 = jnp.bfloat16
_NT = (((1,), (1,)), ((), ()))
_TN = (((0,), (0,)), ((), ()))


def _params(n_axes):
    return pltpu.CompilerParams(
        dimension_semantics=("arbitrary",) * n_axes,
        vmem_limit_bytes=V7X_VMEM_LIMIT,
    )


def _rms(x, g):
    ms = jnp.mean(x * x, axis=-1, keepdims=True)
    return x * lax.rsqrt(ms + EPS) * g


def _prenorm_kernel(n_prompt_blocks, xp_ref, xs_ref, g_ref, h_ref):
    i = pl.program_id(0)

    def body(x_ref):
        h_ref[...] = _rms(x_ref[...], g_ref[...]).astype(BF16)

    pl.when(i < n_prompt_blocks)(lambda: body(xp_ref))
    pl.when(i >= n_prompt_blocks)(lambda: body(xs_ref))


def _prenorm(xp, xs, g, tm):
    n_p, d = xp.shape
    n_s = xs.shape[0]
    npb, nsb = n_p // tm, n_s // tm
    return pl.pallas_call(
        functools.partial(_prenorm_kernel, npb),
        grid=(npb + nsb,),
        in_specs=[
            pl.BlockSpec((tm, d), lambda i: (jnp.minimum(i, npb - 1), 0)),
            pl.BlockSpec((tm, d), lambda i: (jnp.maximum(i - npb, 0), 0)),
            pl.BlockSpec((1, d), lambda i: (0, 0)),
        ],
        out_specs=pl.BlockSpec((tm, d), lambda i: (i, 0)),
        out_shape=jax.ShapeDtypeStruct((n_p + n_s, d), BF16),
        compiler_params=_params(1),
        name="prenorm",
    )(xp, xs, g)


def _inproj_kernel(h_ref, w_ref, lb_ref, u_ref, szc_ref, q_ref, lf_ref, k_ref,
                   v_ref, szh_ref, sgc_ref, sgh_ref):
    h = h_ref[...]

    def mm(g):
        return jnp.dot(h, w_ref[0, :, g * SLAB:(g + 1) * SLAB],
                       preferred_element_type=F32)

    u_ref[...] = (mm(0) * jax.nn.sigmoid(mm(1))).astype(BF16)
    zc = mm(2)
    szc_ref[...] = (zc * jax.nn.sigmoid(zc)).astype(BF16)
    q_ref[...] = mm(3).astype(BF16)

    fa = mm(4)
    lb = lb_ref[...]
    softplus_neg = jnp.log1p(jnp.exp(-jnp.abs(fa)))
    log_sig = jnp.minimum(fa, 0.0) - softplus_neg
    log_sig_neg = -jnp.maximum(fa, 0.0) - softplus_neg
    la = jnp.log(lb)
    y = jnp.log1p(-lb) + log_sig
    lf_ref[...] = jnp.maximum(la, y) + jnp.log1p(jnp.exp(-jnp.abs(la - y)))
    k_ref[...] = ((1.0 - lb) * jnp.exp(log_sig_neg)).astype(BF16)

    v_ref[...] = mm(5).astype(BF16)
    zh = mm(6)
    szh_ref[...] = (zh * jax.nn.sigmoid(zh)).astype(BF16)
    sgc_ref[...] = jax.nn.sigmoid(mm(7)).astype(BF16)
    sgh_ref[...] = jax.nn.sigmoid(mm(8)).astype(BF16)


def _inproj(h, w_slabs, lb, tm):
    n, d = h.shape
    n_slabs = w_slabs.shape[0]
    out_spec = pl.BlockSpec((tm, SLAB), lambda j, i: (i, j))
    bf = jax.ShapeDtypeStruct((n, d), BF16)
    f32 = jax.ShapeDtypeStruct((n, d), F32)
    return pl.pallas_call(
        _inproj_kernel,
        grid=(n_slabs, n // tm),
        in_specs=[
            pl.BlockSpec((tm, d), lambda j, i: (i, 0)),
            pl.BlockSpec((1, d, NUM_GROUPS * SLAB), lambda j, i: (j, 0, 0)),
            pl.BlockSpec((1, SLAB), lambda j, i: (0, j)),
        ],
        out_specs=[out_spec] * 9,
        out_shape=[bf, bf, bf, f32, bf, bf, bf, bf, bf],
        compiler_params=_params(2),
        name="inproj",
    )(h, w_slabs, lb)


def _conv_kernel(n_seg, seg, rb, has_hist, *refs):
    if has_hist:
        (u_ref, szc_ref, sgc_ref, hist_ref, cw_ref, cb_ref, lng_ref, lnb_ref, wpc_ref,
         mc_ref, ubuf, cbuf) = refs
    else:
        (u_ref, szc_ref, sgc_ref, cw_ref, cb_ref, lng_ref, lnb_ref, wpc_ref,
         mc_ref, ubuf, cbuf) = refs
        hist_ref = None
    d = u_ref.shape[-1]
    t = pl.program_id(1)
    stride = seg + HIST_ROWS

    @pl.when(t == 0)
    def _():
        for s in range(n_seg):
            if has_hist:
                ubuf[s * stride:s * stride + HIST_ROWS, :] = hist_ref[s]
            else:
                ubuf[s * stride:s * stride + HIST_ROWS, :] = jnp.zeros((HIST_ROWS, d), F32)

    for s in range(n_seg):
        ubuf[s * stride + HIST_ROWS:(s + 1) * stride, :] = (
            u_ref[s * seg:(s + 1) * seg, :].astype(F32))

    first = HIST_ROWS - (CONV_WIDTH - 1)

    def col_body(ci, carry):
        cols = pl.ds(pl.multiple_of(ci * 128, 128), 128)
        for s in range(n_seg):
            for r0 in range(0, seg, rb):
                base = s * stride + r0 + first
                acc = jnp.zeros((rb, 128), F32)
                for j in range(CONV_WIDTH):
                    acc = acc + cw_ref[j:j + 1, cols] * ubuf[base + j:base + j + rb, cols]
                cbuf[s * seg + r0:s * seg + r0 + rb, cols] = acc + cb_ref[:, cols]
        return carry

    lax.fori_loop(0, d // 128, col_body, 0)

    c = cbuf[...]
    mu = jnp.mean(c, axis=-1, keepdims=True)
    cc = c - mu
    var = jnp.mean(cc * cc, axis=-1, keepdims=True)
    y = cc * lax.rsqrt(var + EPS) * lng_ref[...] + lnb_ref[...]
    a = (y * jax.nn.sigmoid(y) * szc_ref[...].astype(F32)).astype(BF16)
    yc = jnp.dot(a, wpc_ref[...], preferred_element_type=F32)
    mc_ref[...] = (sgc_ref[...].astype(F32) * yc).astype(BF16)

    if not has_hist:
        ubuf[0:HIST_ROWS, :] = ubuf[seg:seg + HIST_ROWS, :]


def _conv_branch(u, szc, sgc, hist, cw, cb, lng, lnb, wpc, *, n_seq, seq_len, row0, name):
    d = u.shape[-1]
    if hist is None:
        n_seg, seg = 1, min(seq_len, 256)
        nt = seq_len // seg
        grid = (n_seq, nt)
    else:
        seg = seq_len
        n_seg = math.gcd(n_seq, max(1, 256 // seg))
        nt = 1
        grid = (n_seq // n_seg, 1)
    tt = n_seg * seg
    rb = min(seg, 64)
    rb0 = row0 // tt
    tok = pl.BlockSpec((tt, d), lambda b, t: (rb0 + b * nt + t, 0))
    vec = pl.BlockSpec((1, d), lambda b, t: (0, 0))
    in_specs = [tok, tok, tok]
    args = [u, szc, sgc]
    if hist is not None:
        in_specs.append(pl.BlockSpec((n_seg, HIST_ROWS, d), lambda b, t: (b, 0, 0)))
        args.append(hist)
    in_specs += [pl.BlockSpec((CONV_WIDTH, d), lambda b, t: (0, 0)), vec, vec, vec,
                 pl.BlockSpec((d, d), lambda b, t: (0, 0), pipeline_mode=pl.Buffered(1))]
    args += [cw, cb, lng, lnb, wpc]
    return pl.pallas_call(
        functools.partial(_conv_kernel, n_seg, seg, rb, hist is not None),
        grid=grid,
        in_specs=in_specs,
        out_specs=pl.BlockSpec((tt, d), lambda b, t: (b * nt + t, 0)),
        out_shape=jax.ShapeDtypeStruct((n_seq * seq_len, d), BF16),
        scratch_shapes=[pltpu.VMEM((n_seg * (seg + HIST_ROWS), d), F32),
                        pltpu.VMEM((tt, d), F32)],
        compiler_params=_params(2),
        name=name,
    )(*args)


def _hgrn_tables(blk):
    n_levels = int(math.log2(blk))
    r = np.arange(blk)[:, None]
    j = np.arange(blk)[None, :]
    mats = [j <= r, j > r]
    masks = [r == j]
    for lvl in range(n_levels):
        c = blk >> (lvl + 1)
        second = ((r // c) % 2) == 1
        ref = (r // (2 * c)) * 2 * c + c - 1
        mats.append(np.where(second, (j > ref) & (j <= r), (j > r) & (j <= ref)))
        first_s = ((j // c) % 2) == 0
        masks.append(((r // (2 * c)) == (j // (2 * c))) & second & first_s)
    m_all = np.concatenate(mats, axis=0).astype(np.float32)
    kpad = -(-3 * blk // 128) * 128
    m3 = np.zeros((m_all.shape[0], kpad), np.float32)
    for s in range(3):
        m3[:, s * blk:(s + 1) * blk] = m_all
    return n_levels, m3, np.stack(masks).astype(np.float32)


def _hgrn_kernel(blk, n_blocks, n_levels, has_s0, *refs):
    if has_s0:
        (q_ref, k_ref, v_ref, lf_ref, szh_ref, g_ref, m3_ref, mask_ref, s0_ref,
         oz_ref, sfin_ref, st_ref) = refs
    else:
        (q_ref, k_ref, v_ref, lf_ref, szh_ref, g_ref, m3_ref, mask_ref,
         oz_ref, sfin_ref, st_ref) = refs
        s0_ref = None
    n_heads = st_ref.shape[0]
    gw = n_heads * HEAD_DIM
    kpad = m3_ref.shape[1]
    t = pl.program_id(2)

    @pl.when(t == 0)
    def _():
        for hh in range(n_heads):
            if has_s0:
                st_ref[hh] = s0_ref[0, hh].T
            else:
                st_ref[hh] = jnp.zeros((HEAD_DIM, HEAD_DIM), F32)

    def block(bi, carry):
        rows = pl.ds(pl.multiple_of(bi * blk, blk), blk)
        lf = lf_ref[rows, :]
        hi = lf.astype(BF16)
        r1 = lf - hi.astype(F32)
        mid = r1.astype(BF16)
        lo = (r1 - mid.astype(F32)).astype(BF16)
        parts = [hi, mid, lo]
        if kpad > 3 * blk:
            parts.append(jnp.zeros((kpad - 3 * blk, gw), BF16))
        lf3 = jnp.concatenate(parts, axis=0)
        e32 = jnp.exp(jnp.dot(m3_ref[...], lf3, preferred_element_type=F32))
        e16 = e32.astype(BF16)
        for hh in range(n_heads):
            ls = slice(hh * HEAD_DIM, (hh + 1) * HEAD_DIM)
            q = q_ref[rows, ls]
            k = k_ref[rows, ls]
            v = v_ref[rows, ls]
            st = st_ref[hh]
            o = lax.dot_general(q * e16[0:blk, ls], st.astype(BF16), _NT,
                                preferred_element_type=F32)
            sc = lax.dot_general(q, k, _NT, preferred_element_type=F32) * mask_ref[0]
            for lvl in range(n_levels):
                ec = e16[(2 + lvl) * blk:(3 + lvl) * blk, ls]
                sc = sc + lax.dot_general(q * ec, k * ec, _NT,
                                          preferred_element_type=F32) * mask_ref[1 + lvl]
            o = o + jnp.dot(sc.astype(BF16), v, preferred_element_type=F32)
            ku = k * e16[blk:2 * blk, ls]
            st_ref[hh] = st * e32[blk - 1:blk, ls] + lax.dot_general(
                v, ku, _TN, preferred_element_type=F32)
            on = _rms(o, g_ref[:, ls])
            oz_ref[rows, ls] = (on * szh_ref[rows, ls].astype(F32)).astype(BF16)
        return carry

    lax.fori_loop(0, n_blocks, block, 0)

    @pl.when(t == pl.num_programs(2) - 1)
    def _():
        for hh in range(n_heads):
            sfin_ref[0, hh] = st_ref[hh].T


def _hgrn_branch(q, k, v, lf, szh, g, s0, *, n_seq, seq_len, row0, name):
    d = q.shape[-1]
    n_heads_total = d // HEAD_DIM
    gw = SLAB
    hpg = gw // HEAD_DIM
    blk = min(HGRN_CHUNK, seq_len)
    tt = min(seq_len, 256)
    nt = seq_len // tt
    n_levels, m3, masks = _hgrn_tables(blk)
    rb0 = row0 // tt
    tok = pl.BlockSpec((tt, gw), lambda b, g_, t: (rb0 + b * nt + t, g_))
    state = pl.BlockSpec((1, hpg, HEAD_DIM, HEAD_DIM), lambda b, g_, t: (b, g_, 0, 0))
    in_specs = [tok, tok, tok, tok, tok,
                pl.BlockSpec((1, gw), lambda b, g_, t: (0, g_)),
                pl.BlockSpec(m3.shape, lambda b, g_, t: (0, 0)),
                pl.BlockSpec(masks.shape, lambda b, g_, t: (0, 0, 0))]
    args = [q, k, v, lf, szh, g, jnp.asarray(m3, BF16), jnp.asarray(masks, F32)]
    if s0 is not None:
        in_specs.append(state)
        args.append(s0)
    return pl.pallas_call(
        functools.partial(_hgrn_kernel, blk, tt // blk, n_levels, s0 is not None),
        grid=(n_seq, d // gw, nt),
        in_specs=in_specs,
        out_specs=[pl.BlockSpec((tt, gw), lambda b, g_, t: (b * nt + t, g_)), state],
        out_shape=[jax.ShapeDtypeStruct((n_seq * seq_len, d), BF16),
                   jax.ShapeDtypeStruct((n_seq, n_heads_total, HEAD_DIM, HEAD_DIM), F32)],
        scratch_shapes=[pltpu.VMEM((hpg, HEAD_DIM, HEAD_DIM), F32)],
        compiler_params=_params(3),
        name=name,
    )(*args)


def _hgrn_tile_tables(blk, sub):
    n_fine, m3, _ = _hgrn_tables(blk)
    r = np.arange(sub)[:, None]
    j = np.arange(sub)[None, :]
    masks = [r == j]
    c = sub // 2
    while c >= 1:
        second = ((r // c) % 2) == 1
        first_s = ((j // c) % 2) == 0
        masks.append(((r // (2 * c)) == (j // (2 * c))) & second & first_s)
        c //= 2
    return n_fine, m3, np.stack(masks).astype(np.float32)


def _hgrn_tile_kernel(blk, sub, n_pairs, n_fine, q_ref, k_ref, v_ref, lf_ref, szh_ref, g_ref,
                      m3_ref, mask_ref, oz_ref, sfin_ref, st_ref):
    n_heads = st_ref.shape[0]
    gw = n_heads * HEAD_DIM
    kpad = m3_ref.shape[1]
    t = pl.program_id(2)

    @pl.when(t == 0)
    def _():
        st_ref[...] = jnp.zeros(st_ref.shape, F32)

    def decays(rows):
        lf = lf_ref[rows, :]
        hi = lf.astype(BF16)
        r1 = lf - hi.astype(F32)
        mid = r1.astype(BF16)
        lo = (r1 - mid.astype(F32)).astype(BF16)
        parts = [hi, mid, lo]
        if kpad > 3 * blk:
            parts.append(jnp.zeros((kpad - 3 * blk, gw), BF16))
        lf3 = jnp.concatenate(parts, axis=0)
        return jnp.exp(jnp.dot(m3_ref[...], lf3, preferred_element_type=F32))

    def subtile(r0):
        e_a = decays(pl.ds(r0, blk))
        e_b = decays(pl.ds(pl.multiple_of(r0 + blk, blk), blk))
        rows = pl.ds(r0, sub)
        for hh in range(n_heads):
            ls = slice(hh * HEAD_DIM, (hh + 1) * HEAD_DIM)
            q = q_ref[rows, ls]
            k = k_ref[rows, ls]
            v = v_ref[rows, ls]
            p_a, r_a = e_a[0:blk, ls], e_a[blk:2 * blk, ls]
            p_b, r_b = e_b[0:blk, ls], e_b[blk:2 * blk, ls]
            d_a, d_b = p_a[blk - 1:blk, :], p_b[blk - 1:blk, :]

            sc = lax.dot_general(q, k, _NT, preferred_element_type=F32) * mask_ref[0]
            e_x = jnp.concatenate([r_a, p_b], axis=0).astype(BF16)
            sc = sc + lax.dot_general(q * e_x, k * e_x, _NT,
                                      preferred_element_type=F32) * mask_ref[1]
            for lvl in range(n_fine):
                sl = slice((2 + lvl) * blk, (3 + lvl) * blk)
                ec = jnp.concatenate([e_a[sl, ls], e_b[sl, ls]], axis=0).astype(BF16)
                sc = sc + lax.dot_general(q * ec, k * ec, _NT,
                                          preferred_element_type=F32) * mask_ref[2 + lvl]

            e_in = jnp.concatenate([p_a, p_b * d_a], axis=0).astype(BF16)
            e_up = jnp.concatenate([r_a * d_b, r_b], axis=0).astype(BF16)
            st = st_ref[hh]
            o = lax.dot_general(q * e_in, st.astype(BF16), _NT, preferred_element_type=F32)
            o = o + jnp.dot(sc.astype(BF16), v, preferred_element_type=F32)
            st_ref[hh] = st * (d_a * d_b) + lax.dot_general(
                v, k * e_up, _TN, preferred_element_type=F32)
            on = _rms(o, g_ref[:, ls])
            oz_ref[rows, ls] = (on * szh_ref[rows, ls].astype(F32)).astype(BF16)

    def pair(pi, carry):
        r0 = pl.multiple_of(pi * (2 * sub), 2 * sub)
        subtile(r0)
        subtile(pl.multiple_of(r0 + sub, sub))
        return carry

    lax.fori_loop(0, n_pairs, pair, 0)

    @pl.when(t == pl.num_programs(2) - 1)
    def _():
        for hh in range(n_heads):
            sfin_ref[0, hh] = st_ref[hh].T


def _hgrn_long(q, k, v, lf, szh, g, *, n_seq, seq_len, name):
    d = q.shape[-1]
    gw = SLAB
    hpg = gw // HEAD_DIM
    blk, sub = HGRN_CHUNK, 2 * HGRN_CHUNK
    tt = 512 if seq_len % 512 == 0 else 2 * sub
    nt = seq_len // tt
    n_fine, m3, masks = _hgrn_tile_tables(blk, sub)
    tok = pl.BlockSpec((tt, gw), lambda b, g_, t: (b * nt + t, g_))
    state = pl.BlockSpec((1, hpg, HEAD_DIM, HEAD_DIM), lambda b, g_, t: (b, g_, 0, 0))
    return pl.pallas_call(
        functools.partial(_hgrn_tile_kernel, blk, sub, tt // (2 * sub), n_fine),
        grid=(n_seq, d // gw, nt),
        in_specs=[tok, tok, tok, tok, tok,
                  pl.BlockSpec((1, gw), lambda b, g_, t: (0, g_)),
                  pl.BlockSpec(m3.shape, lambda b, g_, t: (0, 0)),
                  pl.BlockSpec(masks.shape, lambda b, g_, t: (0, 0, 0))],
        out_specs=[tok, state],
        out_shape=[jax.ShapeDtypeStruct((n_seq * seq_len, d), BF16),
                   jax.ShapeDtypeStruct((n_seq, d // HEAD_DIM, HEAD_DIM, HEAD_DIM), F32)],
        scratch_shapes=[pltpu.VMEM((hpg, HEAD_DIM, HEAD_DIM), F32)],
        compiler_params=_params(3),
        name=name,
    )(q, k, v, lf, szh, g, jnp.asarray(m3, BF16), jnp.asarray(masks, F32))


def _out_kernel(n_prompt_blocks, emit_h, *refs):
    (mcp_ref, mcs_ref, ozp_ref, ozs_ref, sgh_ref, xp_ref, xs_ref,
     wph_ref, wout_ref, gpost_ref, gnext_ref) = refs[:11]
    if emit_h:
        yp_ref, ys_ref, hn_ref = refs[11:]
    else:
        yp_ref, ys_ref = refs[11:]
        hn_ref = None
    i = pl.program_id(0)

    def body(mc_ref, oz_ref, x_ref, y_ref):
        yh = jnp.dot(oz_ref[...], wph_ref[...], preferred_element_type=F32)
        m = mc_ref[...].astype(F32) + sgh_ref[...].astype(F32) * yh
        o = jnp.dot(m.astype(BF16), wout_ref[...], preferred_element_type=F32)
        y = x_ref[...] + _rms(o, gpost_ref[...])
        y_ref[...] = y
        if emit_h:
            hn_ref[...] = _rms(y, gnext_ref[...]).astype(BF16)

    pl.when(i < n_prompt_blocks)(lambda: body(mcp_ref, ozp_ref, xp_ref, yp_ref))
    pl.when(i >= n_prompt_blocks)(lambda: body(mcs_ref, ozs_ref, xs_ref, ys_ref))


def _out_proj(mc_p, mc_s, oz_p, oz_s, sgh, xp, xs, wph, wout, gpost, gnext,
              *, tm, emit_h):
    n_p, d = xp.shape
    n_s = xs.shape[0]
    npb, nsb = n_p // tm, n_s // tm
    pr = pl.BlockSpec((tm, d), lambda i: (jnp.minimum(i, npb - 1), 0))
    sa = pl.BlockSpec((tm, d), lambda i: (jnp.maximum(i - npb, 0), 0))
    al = pl.BlockSpec((tm, d), lambda i: (i, 0))
    wt = pl.BlockSpec((d, d), lambda i: (0, 0), pipeline_mode=pl.Buffered(1))
    vec = pl.BlockSpec((1, d), lambda i: (0, 0))
    out_specs = [pr, sa]
    out_shape = [jax.ShapeDtypeStruct((n_p, d), F32), jax.ShapeDtypeStruct((n_s, d), F32)]
    if emit_h:
        out_specs.append(al)
        out_shape.append(jax.ShapeDtypeStruct((n_p + n_s, d), BF16))
    return pl.pallas_call(
        functools.partial(_out_kernel, npb, emit_h),
        grid=(npb + nsb,),
        in_specs=[pr, sa, pr, sa, al, pr, sa, wt, wt, vec, vec],
        out_specs=out_specs,
        out_shape=out_shape,
        compiler_params=_params(1),
        name="outproj",
    )(mc_p, mc_s, oz_p, oz_s, sgh, xp, xs, wph, wout, gpost, gnext)


def kernel(x_prompt, x_sample, cache_conv, state_hgrn, lb_logits, w_in, conv_w, conv_b,
           conv_ln_g, conv_ln_b, hg_norm_g, w_pc, w_ph, w_out, norm_pre_g, norm_post_g):
    bp, tp, d = x_prompt.shape
    bs, ts, _ = x_sample.shape
    depth = w_in.shape[0]
    n_p, n_s = bp * tp, bs * ts
    n_slabs = d // SLAB
    assert w_in.shape[-1] == NUM_GROUPS * d and d % SLAB == 0
    assert ts >= CONV_WIDTH - 1 and ts <= HGRN_CHUNK and tp % 256 == 0
    tm_in = math.gcd(512, math.gcd(n_p, n_s))
    tm_out = math.gcd(256, math.gcd(n_p, n_s))

    lb_all = jnp.cumsum(jax.nn.softmax(lb_logits.astype(F32), axis=0), axis=0)
    lb_all = lb_all - lb_all[0:1]

    xp = x_prompt.reshape(n_p, d)
    xs = x_sample.reshape(n_s, d)
    hist = jnp.pad(cache_conv, ((0, 0), (0, 0), (HIST_ROWS - (CONV_WIDTH - 1), 0), (0, 0)))

    row = lambda a, l: a[l].reshape(1, d)
    h = _prenorm(xp, xs, row(norm_pre_g, 0), tm_in)
    conv_p, hg_p, conv_s, hg_s = [], [], [], []
    for l in range(depth):
        w_slabs = (w_in[l].reshape(d, NUM_GROUPS, n_slabs, SLAB).transpose(2, 0, 1, 3)
                   .reshape(n_slabs, d, NUM_GROUPS * SLAB).astype(BF16))
        u, szc, q, lf, k, v, szh, sgc, sgh = _inproj(h, w_slabs, lb_all[l].reshape(1, d), tm_in)

        conv_args = (conv_w[l], row(conv_b, l), row(conv_ln_g, l), row(conv_ln_b, l),
                     w_pc[l].astype(BF16))
        mc_p = _conv_branch(u, szc, sgc, None, *conv_args, n_seq=bp, seq_len=tp, row0=0,
                            name="conv_prompt")
        mc_s = _conv_branch(u, szc, sgc, hist[l], *conv_args, n_seq=bs, seq_len=ts, row0=n_p,
                            name="conv_sample")
        g_h = row(hg_norm_g, l)
        oz_p, sf_p = _hgrn_long(q, k, v, lf, szh, g_h, n_seq=bp, seq_len=tp, name="hgrn_prompt")
        oz_s, sf_s = _hgrn_branch(q, k, v, lf, szh, g_h, state_hgrn[l], n_seq=bs, seq_len=ts,
                                  row0=n_p, name="hgrn_sample")

        last = l == depth - 1
        outs = _out_proj(mc_p, mc_s, oz_p, oz_s, sgh, xp, xs,
                         w_ph[l].astype(BF16), w_out[l].astype(BF16),
                         row(norm_post_g, l), row(norm_pre_g, min(l + 1, depth - 1)),
                         tm=tm_out, emit_h=not last)
        if last:
            xp, xs = outs
        else:
            xp, xs, h = outs

        nb = CONV_WIDTH - 1
        conv_p.append(u[:n_p].reshape(bp, tp, d)[:, tp - nb:].astype(F32))
        conv_s.append(u[n_p:].reshape(bs, ts, d)[:, ts - nb:].astype(F32))
        hg_p.append(sf_p)
        hg_s.append(sf_s)

    return (xp.reshape(bp, tp, d), xs.reshape(bs, ts, d),
            jnp.stack(conv_p), jnp.stack(hg_p), jnp.stack(conv_s), jnp.stack(hg_s))
```

```python
import functools
import math

import numpy as np
import jax
import jax.numpy as jnp
from jax import lax
from jax.experimental import pallas as pl
from jax.experimental.pallas import tpu as pltpu

EPS = 1e-6
HEAD_DIM = 128
CONV_WIDTH = 31
HIST_ROWS = 32
NUM_GROUPS = 9
SLAB = 256
HGRN_CHUNK = 64
V7X_VMEM_LIMIT = 56 * 1024 * 1024

F32 = jnp.float32
BF16 = jnp.bfloat16
_NT = (((1,), (1,)), ((), ()))
_TN = (((0,), (0,)), ((), ()))


def _params(n_axes, flags=None):
    return pltpu.CompilerParams(
        dimension_semantics=("arbitrary",) * n_axes,
        vmem_limit_bytes=V7X_VMEM_LIMIT,
        flags=flags,
    )


def _rms(x, g):
    ms = jnp.mean(x * x, axis=-1, keepdims=True)
    return x * lax.rsqrt(ms + EPS) * g


def _prenorm_kernel(n_prompt_blocks, xp_ref, xs_ref, g_ref, h_ref):
    i = pl.program_id(0)

    def body(x_ref):
        h_ref[...] = _rms(x_ref[...], g_ref[...]).astype(BF16)

    pl.when(i < n_prompt_blocks)(lambda: body(xp_ref))
    pl.when(i >= n_prompt_blocks)(lambda: body(xs_ref))


def _prenorm(xp, xs, g, tm):
    n_p, d = xp.shape
    n_s = xs.shape[0]
    npb, nsb = n_p // tm, n_s // tm
    return pl.pallas_call(
        functools.partial(_prenorm_kernel, npb),
        grid=(npb + nsb,),
        in_specs=[
            pl.BlockSpec((tm, d), lambda i: (jnp.minimum(i, npb - 1), 0)),
            pl.BlockSpec((tm, d), lambda i: (jnp.maximum(i - npb, 0), 0)),
            pl.BlockSpec((1, d), lambda i: (0, 0)),
        ],
        out_specs=pl.BlockSpec((tm, d), lambda i: (i, 0)),
        out_shape=jax.ShapeDtypeStruct((n_p + n_s, d), BF16),
        compiler_params=_params(1),
        name="prenorm",
    )(xp, xs, g)


_CONV_FIRST = HIST_ROWS - (CONV_WIDTH - 1)


def _conv_rows(ubuf, cw_ref, r0, rb, cols):
    acc = None
    for res in range(8):
        ext = rb + (8 if res else 0)
        grp = None
        for j in range(CONV_WIDTH):
            off = _CONV_FIRST + j
            if off % 8 != res:
                continue
            base = r0 + off - res
            term = cw_ref[j:j + 1, cols] * ubuf[base:base + ext, cols]
            grp = term if grp is None else grp + term
        part = grp[res:res + rb] if res else grp
        acc = part if acc is None else acc + part
    return acc


def _inproj_kernel(h_ref, *refs):
    w_refs = refs[:NUM_GROUPS]
    (lb_ref, u_ref, szc_ref, q_ref, lf_ref, k_ref, v_ref, szh_ref, sgc_ref,
     sgh_ref) = refs[NUM_GROUPS:]
    h = h_ref[...]

    def mm(g):
        return jnp.dot(h, w_refs[g][...], preferred_element_type=F32)

    u_ref[...] = (mm(0) * jax.nn.sigmoid(mm(1))).astype(BF16)
    zc = mm(2)
    szc_ref[...] = (zc * jax.nn.sigmoid(zc)).astype(BF16)
    q_ref[...] = mm(3).astype(BF16)

    fa = mm(4)
    lb = lb_ref[...]
    softplus_neg = jnp.log1p(jnp.exp(-jnp.abs(fa)))
    log_sig = jnp.minimum(fa, 0.0) - softplus_neg
    log_sig_neg = -jnp.maximum(fa, 0.0) - softplus_neg
    la = jnp.log(lb)
    y = jnp.log1p(-lb) + log_sig
    lf_ref[...] = jnp.maximum(la, y) + jnp.log1p(jnp.exp(-jnp.abs(la - y)))
    k_ref[...] = ((1.0 - lb) * jnp.exp(log_sig_neg)).astype(BF16)

    v_ref[...] = mm(5).astype(BF16)
    zh = mm(6)
    szh_ref[...] = (zh * jax.nn.sigmoid(zh)).astype(BF16)
    sgc_ref[...] = jax.nn.sigmoid(mm(7)).astype(BF16)
    sgh_ref[...] = jax.nn.sigmoid(mm(8)).astype(BF16)


def _inproj(h, w, lb, tm):
    n, d = h.shape
    n_slabs = d // SLAB
    out_spec = pl.BlockSpec((tm, SLAB), lambda j, i: (i, j))
    w_specs = [pl.BlockSpec((d, SLAB), functools.partial(lambda j, i, g: (0, g * n_slabs + j), g=g))
               for g in range(NUM_GROUPS)]
    bf = jax.ShapeDtypeStruct((n, d), BF16)
    f32 = jax.ShapeDtypeStruct((n, d), F32)
    return pl.pallas_call(
        _inproj_kernel,
        grid=(n_slabs, n // tm),
        in_specs=[pl.BlockSpec((tm, d), lambda j, i: (i, 0))] + w_specs
        + [pl.BlockSpec((1, SLAB), lambda j, i: (0, j))],
        out_specs=[out_spec] * 9,
        out_shape=[bf, bf, bf, f32, bf, bf, bf, bf, bf],
        compiler_params=_params(2),
        name="inproj",
    )(h, *([w] * NUM_GROUPS), lb)


def _conv_tail(c, szc, sgc, lng_ref, lnb_ref, wpc_ref):
    mu = jnp.mean(c, axis=-1, keepdims=True)
    cc = c - mu
    var = jnp.mean(cc * cc, axis=-1, keepdims=True)
    y = cc * lax.rsqrt(var + EPS) * lng_ref[...] + lnb_ref[...]
    a = (y * jax.nn.sigmoid(y) * szc.astype(F32)).astype(BF16)
    yc = jnp.dot(a, wpc_ref[...], preferred_element_type=F32)
    return (sgc.astype(F32) * yc).astype(BF16)


def _conv_long_kernel(tt, rb, u_ref, szc_ref, sgc_ref, cw_ref, cb_ref, lng_ref, lnb_ref,
                      wpc_ref, mc_ref, ubuf, cbuf):
    d = u_ref.shape[-1]

    @pl.when(pl.program_id(1) == 0)
    def _():
        ubuf[0:HIST_ROWS, :] = jnp.zeros((HIST_ROWS, d), F32)

    ubuf[HIST_ROWS:HIST_ROWS + tt, :] = u_ref[...].astype(F32)

    def col_body(ci, carry):
        cols = pl.ds(pl.multiple_of(ci * 128, 128), 128)
        for r0 in range(0, tt, rb):
            cbuf[r0:r0 + rb, cols] = _conv_rows(ubuf, cw_ref, r0, rb, cols) + cb_ref[:, cols]
        return carry

    lax.fori_loop(0, d // 128, col_body, 0)
    mc_ref[...] = _conv_tail(cbuf[...], szc_ref[...], sgc_ref[...], lng_ref, lnb_ref, wpc_ref)
    ubuf[0:HIST_ROWS, :] = ubuf[tt:tt + HIST_ROWS, :]


def _conv_long(u, szc, sgc, cw, cb, lng, lnb, wpc, *, n_seq, seq_len):
    d = u.shape[-1]
    tt = 256
    nt = seq_len // tt
    tok = pl.BlockSpec((tt, d), lambda b, t: (b * nt + t, 0))
    vec = pl.BlockSpec((1, d), lambda b, t: (0, 0))
    return pl.pallas_call(
        functools.partial(_conv_long_kernel, tt, 64),
        grid=(n_seq, nt),
        in_specs=[tok, tok, tok, pl.BlockSpec((CONV_WIDTH, d), lambda b, t: (0, 0)),
                  vec, vec, vec,
                  pl.BlockSpec((d, d), lambda b, t: (0, 0), pipeline_mode=pl.Buffered(1))],
        out_specs=tok,
        out_shape=jax.ShapeDtypeStruct((n_seq * seq_len, d), BF16),
        scratch_shapes=[pltpu.VMEM((tt + HIST_ROWS, d), F32), pltpu.VMEM((tt, d), F32)],
        compiler_params=_params(2),
        name="conv_prompt",
    )(u, szc, sgc, cw, cb, lng, lnb, wpc)


def _conv_short_kernel(n_seg, seg, u_ref, szc_ref, sgc_ref, hist_ref, cw_ref, cb_ref,
                       lng_ref, lnb_ref, wpc_ref, mc_ref, ubuf, cbuf):
    d = u_ref.shape[-1]
    stride = seg + HIST_ROWS
    for s in range(n_seg):
        ubuf[s * stride:s * stride + HIST_ROWS, :] = hist_ref[s]
        ubuf[s * stride + HIST_ROWS:(s + 1) * stride, :] = (
            u_ref[s * seg:(s + 1) * seg, :].astype(F32))

    def col_body(ci, carry):
        cols = pl.ds(pl.multiple_of(ci * 128, 128), 128)
        for s in range(n_seg):
            cbuf[s * seg:(s + 1) * seg, cols] = (
                _conv_rows(ubuf, cw_ref, s * stride, seg, cols) + cb_ref[:, cols])
        return carry

    lax.fori_loop(0, d // 128, col_body, 0)
    mc_ref[...] = _conv_tail(cbuf[...], szc_ref[...], sgc_ref[...], lng_ref, lnb_ref, wpc_ref)


def _conv_short(u, szc, sgc, hist, cw, cb, lng, lnb, wpc, *, n_seq, seq_len, row0):
    d = u.shape[-1]
    n_seg = math.gcd(n_seq, max(1, 256 // seq_len))
    tt = n_seg * seq_len
    rb0 = row0 // tt
    tok = pl.BlockSpec((tt, d), lambda b: (rb0 + b, 0))
    vec = pl.BlockSpec((1, d), lambda b: (0, 0))
    return pl.pallas_call(
        functools.partial(_conv_short_kernel, n_seg, seq_len),
        grid=(n_seq // n_seg,),
        in_specs=[tok, tok, tok,
                  pl.BlockSpec((n_seg, HIST_ROWS, d), lambda b: (b, 0, 0)),
                  pl.BlockSpec((CONV_WIDTH, d), lambda b: (0, 0)), vec, vec, vec,
                  pl.BlockSpec((d, d), lambda b: (0, 0), pipeline_mode=pl.Buffered(1))],
        out_specs=pl.BlockSpec((tt, d), lambda b: (b, 0)),
        out_shape=jax.ShapeDtypeStruct((n_seq * seq_len, d), BF16),
        scratch_shapes=[pltpu.VMEM((n_seg * (seq_len + HIST_ROWS), d), F32),
                        pltpu.VMEM((tt, d), F32)],
        compiler_params=_params(1),
        name="conv_sample",
    )(u, szc, sgc, hist, cw, cb, lng, lnb, wpc)


def _hgrn_tables(blk):
    n_levels = int(math.log2(blk))
    r = np.arange(blk)[:, None]
    j = np.arange(blk)[None, :]
    mats = [j <= r, j > r]
    masks = [r == j]
    for lvl in range(n_levels):
        c = blk >> (lvl + 1)
        second = ((r // c) % 2) == 1
        ref = (r // (2 * c)) * 2 * c + c - 1
        mats.append(np.where(second, (j > ref) & (j <= r), (j > r) & (j <= ref)))
        first_s = ((j // c) % 2) == 0
        masks.append(((r // (2 * c)) == (j // (2 * c))) & second & first_s)
    m_all = np.concatenate(mats, axis=0).astype(np.float32)
    kpad = -(-3 * blk // 128) * 128
    m3 = np.zeros((m_all.shape[0], kpad), np.float32)
    for s in range(3):
        m3[:, s * blk:(s + 1) * blk] = m_all
    return n_levels, m3, np.stack(masks).astype(np.float32)


def _hgrn_kernel(blk, n_blocks, n_levels, has_s0, *refs):
    if has_s0:
        (q_ref, k_ref, v_ref, lf_ref, szh_ref, g_ref, m3_ref, mask_ref, s0_ref,
         oz_ref, sfin_ref, st_ref) = refs
    else:
        (q_ref, k_ref, v_ref, lf_ref, szh_ref, g_ref, m3_ref, mask_ref,
         oz_ref, sfin_ref, st_ref) = refs
        s0_ref = None
    n_heads = st_ref.shape[0]
    gw = n_heads * HEAD_DIM
    kpad = m3_ref.shape[1]
    t = pl.program_id(2)

    @pl.when(t == 0)
    def _():
        for hh in range(n_heads):
            if has_s0:
                st_ref[hh] = s0_ref[0, hh].T
            else:
                st_ref[hh] = jnp.zeros((HEAD_DIM, HEAD_DIM), F32)

    def block(bi, carry):
        rows = pl.ds(pl.multiple_of(bi * blk, blk), blk)
        lf = lf_ref[rows, :]
        hi = lf.astype(BF16)
        r1 = lf - hi.astype(F32)
        mid = r1.astype(BF16)
        lo = (r1 - mid.astype(F32)).astype(BF16)
        parts = [hi, mid, lo]
        if kpad > 3 * blk:
            parts.append(jnp.zeros((kpad - 3 * blk, gw), BF16))
        lf3 = jnp.concatenate(parts, axis=0)
        e32 = jnp.exp(jnp.dot(m3_ref[...], lf3, preferred_element_type=F32))
        e16 = e32.astype(BF16)
        for hh in range(n_heads):
            ls = slice(hh * HEAD_DIM, (hh + 1) * HEAD_DIM)
            q = q_ref[rows, ls]
            k = k_ref[rows, ls]
            v = v_ref[rows, ls]
            st = st_ref[hh]
            o = lax.dot_general(q * e16[0:blk, ls], st.astype(BF16), _NT,
                                preferred_element_type=F32)
            sc = lax.dot_general(q, k, _NT, preferred_element_type=F32) * mask_ref[0]
            for lvl in range(n_levels):
                ec = e16[(2 + lvl) * blk:(3 + lvl) * blk, ls]
                sc = sc + lax.dot_general(q * ec, k * ec, _NT,
                                          preferred_element_type=F32) * mask_ref[1 + lvl]
            o = o + jnp.dot(sc.astype(BF16), v, preferred_element_type=F32)
            ku = k * e16[blk:2 * blk, ls]
            st_ref[hh] = st * e32[blk - 1:blk, ls] + lax.dot_general(
                v, ku, _TN, preferred_element_type=F32)
            on = _rms(o, g_ref[:, ls])
            oz_ref[rows, ls] = (on * szh_ref[rows, ls].astype(F32)).astype(BF16)
        return carry

    lax.fori_loop(0, n_blocks, block, 0)

    @pl.when(t == pl.num_programs(2) - 1)
    def _():
        for hh in range(n_heads):
            sfin_ref[0, hh] = st_ref[hh].T


def _hgrn_branch(q, k, v, lf, szh, g, s0, *, n_seq, seq_len, row0, name):
    d = q.shape[-1]
    n_heads_total = d // HEAD_DIM
    gw = SLAB
    hpg = gw // HEAD_DIM
    blk = min(HGRN_CHUNK, seq_len)
    tt = min(seq_len, 256)
    nt = seq_len // tt
    n_levels, m3, masks = _hgrn_tables(blk)
    rb0 = row0 // tt
    tok = pl.BlockSpec((tt, gw), lambda b, g_, t: (rb0 + b * nt + t, g_))
    state = pl.BlockSpec((1, hpg, HEAD_DIM, HEAD_DIM), lambda b, g_, t: (b, g_, 0, 0))
    in_specs = [tok, tok, tok, tok, tok,
                pl.BlockSpec((1, gw), lambda b, g_, t: (0, g_)),
                pl.BlockSpec(m3.shape, lambda b, g_, t: (0, 0)),
                pl.BlockSpec(masks.shape, lambda b, g_, t: (0, 0, 0))]
    args = [q, k, v, lf, szh, g, jnp.asarray(m3, BF16), jnp.asarray(masks, F32)]
    if s0 is not None:
        in_specs.append(state)
        args.append(s0)
    return pl.pallas_call(
        functools.partial(_hgrn_kernel, blk, tt // blk, n_levels, s0 is not None),
        grid=(n_seq, d // gw, nt),
        in_specs=in_specs,
        out_specs=[pl.BlockSpec((tt, gw), lambda b, g_, t: (b * nt + t, g_)), state],
        out_shape=[jax.ShapeDtypeStruct((n_seq * seq_len, d), BF16),
                   jax.ShapeDtypeStruct((n_seq, n_heads_total, HEAD_DIM, HEAD_DIM), F32)],
        scratch_shapes=[pltpu.VMEM((hpg, HEAD_DIM, HEAD_DIM), F32)],
        compiler_params=_params(3),
        name=name,
    )(*args)


def _hgrn_tile_tables(blk, sub):
    n_fine, m3, _ = _hgrn_tables(blk)
    r = np.arange(sub)[:, None]
    j = np.arange(sub)[None, :]
    masks = [r == j]
    c = sub // 2
    while c >= 1:
        second = ((r // c) % 2) == 1
        first_s = ((j // c) % 2) == 0
        masks.append(((r // (2 * c)) == (j // (2 * c))) & second & first_s)
        c //= 2
    return n_fine, m3, np.stack(masks).astype(np.float32)


def _hgrn_tile_kernel(blk, sub, n_pairs, n_fine, q_ref, k_ref, v_ref, lf_ref, szh_ref, g_ref,
                      m3_ref, mask_ref, oz_ref, sfin_ref, st_ref):
    n_heads = st_ref.shape[0]
    gw = n_heads * HEAD_DIM
    kpad = m3_ref.shape[1]
    t = pl.program_id(2)

    @pl.when(t == 0)
    def _():
        st_ref[...] = jnp.zeros(st_ref.shape, F32)

    def decays(rows):
        lf = lf_ref[rows, :]
        hi = lf.astype(BF16)
        r1 = lf - hi.astype(F32)
        mid = r1.astype(BF16)
        lo = (r1 - mid.astype(F32)).astype(BF16)
        parts = [hi, mid, lo]
        if kpad > 3 * blk:
            parts.append(jnp.zeros((kpad - 3 * blk, gw), BF16))
        lf3 = jnp.concatenate(parts, axis=0)
        return jnp.exp(jnp.dot(m3_ref[...], lf3, preferred_element_type=F32))

    def subtile(r0):
        e_a = decays(pl.ds(r0, blk))
        e_b = decays(pl.ds(pl.multiple_of(r0 + blk, blk), blk))
        rows = pl.ds(r0, sub)
        for hh in range(n_heads):
            ls = slice(hh * HEAD_DIM, (hh + 1) * HEAD_DIM)
            q = q_ref[rows, ls]
            k = k_ref[rows, ls]
            v = v_ref[rows, ls]
            p_a, r_a = e_a[0:blk, ls], e_a[blk:2 * blk, ls]
            p_b, r_b = e_b[0:blk, ls], e_b[blk:2 * blk, ls]
            d_a, d_b = p_a[blk - 1:blk, :], p_b[blk - 1:blk, :]

            sc = lax.dot_general(q, k, _NT, preferred_element_type=F32) * mask_ref[0]
            e_x = jnp.concatenate([r_a, p_b], axis=0).astype(BF16)
            sc = sc + lax.dot_general(q * e_x, k * e_x, _NT,
                                      preferred_element_type=F32) * mask_ref[1]
            for lvl in range(n_fine):
                sl = slice((2 + lvl) * blk, (3 + lvl) * blk)
                ec = jnp.concatenate([e_a[sl, ls], e_b[sl, ls]], axis=0).astype(BF16)
                sc = sc + lax.dot_general(q * ec, k * ec, _NT,
                                          preferred_element_type=F32) * mask_ref[2 + lvl]

            e_in = jnp.concatenate([p_a, p_b * d_a], axis=0).astype(BF16)
            e_up = jnp.concatenate([r_a * d_b, r_b], axis=0).astype(BF16)
            st = st_ref[hh]
            o = lax.dot_general(q * e_in, st.astype(BF16), _NT, preferred_element_type=F32)
            o = o + jnp.dot(sc.astype(BF16), v, preferred_element_type=F32)
            st_ref[hh] = st * (d_a * d_b) + lax.dot_general(
                v, k * e_up, _TN, preferred_element_type=F32)
            on = _rms(o, g_ref[:, ls])
            oz_ref[rows, ls] = (on * szh_ref[rows, ls].astype(F32)).astype(BF16)

    def pair(pi, carry):
        r0 = pl.multiple_of(pi * (2 * sub), 2 * sub)
        subtile(r0)
        subtile(pl.multiple_of(r0 + sub, sub))
        return carry

    lax.fori_loop(0, n_pairs, pair, 0)

    @pl.when(t == pl.num_programs(2) - 1)
    def _():
        for hh in range(n_heads):
            sfin_ref[0, hh] = st_ref[hh].T


def _hgrn_long(q, k, v, lf, szh, g, *, n_seq, seq_len, name):
    d = q.shape[-1]
    gw = SLAB
    hpg = gw // HEAD_DIM
    blk, sub = HGRN_CHUNK, 2 * HGRN_CHUNK
    tt = 512 if seq_len % 512 == 0 else 2 * sub
    nt = seq_len // tt
    n_fine, m3, masks = _hgrn_tile_tables(blk, sub)
    tok = pl.BlockSpec((tt, gw), lambda b, g_, t: (b * nt + t, g_))
    state = pl.BlockSpec((1, hpg, HEAD_DIM, HEAD_DIM), lambda b, g_, t: (b, g_, 0, 0))
    return pl.pallas_call(
        functools.partial(_hgrn_tile_kernel, blk, sub, tt // (2 * sub), n_fine),
        grid=(n_seq, d // gw, nt),
        in_specs=[tok, tok, tok, tok, tok,
                  pl.BlockSpec((1, gw), lambda b, g_, t: (0, g_)),
                  pl.BlockSpec(m3.shape, lambda b, g_, t: (0, 0)),
                  pl.BlockSpec(masks.shape, lambda b, g_, t: (0, 0, 0))],
        out_specs=[tok, state],
        out_shape=[jax.ShapeDtypeStruct((n_seq * seq_len, d), BF16),
                   jax.ShapeDtypeStruct((n_seq, d // HEAD_DIM, HEAD_DIM, HEAD_DIM), F32)],
        scratch_shapes=[pltpu.VMEM((hpg, HEAD_DIM, HEAD_DIM), F32)],
        compiler_params=_params(3),
        name=name,
    )(q, k, v, lf, szh, g, jnp.asarray(m3, BF16), jnp.asarray(masks, F32))


def _out_kernel(n_prompt_blocks, emit_h, *refs):
    (mcp_ref, mcs_ref, ozp_ref, ozs_ref, sgh_ref, xp_ref, xs_ref,
     wph_ref, wout_ref, gpost_ref, gnext_ref) = refs[:11]
    if emit_h:
        yp_ref, ys_ref, hn_ref = refs[11:]
    else:
        yp_ref, ys_ref = refs[11:]
        hn_ref = None
    i = pl.program_id(0)

    def body(mc_ref, oz_ref, x_ref, y_ref):
        yh = jnp.dot(oz_ref[...], wph_ref[...], preferred_element_type=F32)
        m = mc_ref[...].astype(F32) + sgh_ref[...].astype(F32) * yh
        o = jnp.dot(m.astype(BF16), wout_ref[...], preferred_element_type=F32)
        y = x_ref[...] + _rms(o, gpost_ref[...])
        y_ref[...] = y
        if emit_h:
            hn_ref[...] = _rms(y, gnext_ref[...]).astype(BF16)

    pl.when(i < n_prompt_blocks)(lambda: body(mcp_ref, ozp_ref, xp_ref, yp_ref))
    pl.when(i >= n_prompt_blocks)(lambda: body(mcs_ref, ozs_ref, xs_ref, ys_ref))


def _out_proj(mc_p, mc_s, oz_p, oz_s, sgh, xp, xs, wph, wout, gpost, gnext,
              *, tm, emit_h):
    n_p, d = xp.shape
    n_s = xs.shape[0]
    npb, nsb = n_p // tm, n_s // tm
    pr = pl.BlockSpec((tm, d), lambda i: (jnp.minimum(i, npb - 1), 0))
    sa = pl.BlockSpec((tm, d), lambda i: (jnp.maximum(i - npb, 0), 0))
    al = pl.BlockSpec((tm, d), lambda i: (i, 0))
    wt = pl.BlockSpec((d, d), lambda i: (0, 0), pipeline_mode=pl.Buffered(1))
    vec = pl.BlockSpec((1, d), lambda i: (0, 0))
    out_specs = [pr, sa]
    out_shape = [jax.ShapeDtypeStruct((n_p, d), F32), jax.ShapeDtypeStruct((n_s, d), F32)]
    if emit_h:
        out_specs.append(al)
        out_shape.append(jax.ShapeDtypeStruct((n_p + n_s, d), BF16))
    return pl.pallas_call(
        functools.partial(_out_kernel, npb, emit_h),
        grid=(npb + nsb,),
        in_specs=[pr, sa, pr, sa, al, pr, sa, wt, wt, vec, vec],
        out_specs=out_specs,
        out_shape=out_shape,
        compiler_params=_params(1),
        name="outproj",
    )(mc_p, mc_s, oz_p, oz_s, sgh, xp, xs, wph, wout, gpost, gnext)


def kernel(x_prompt, x_sample, cache_conv, state_hgrn, lb_logits, w_in, conv_w, conv_b,
           conv_ln_g, conv_ln_b, hg_norm_g, w_pc, w_ph, w_out, norm_pre_g, norm_post_g):
    bp, tp, d = x_prompt.shape
    bs, ts, _ = x_sample.shape
    depth = w_in.shape[0]
    n_p, n_s = bp * tp, bs * ts
    assert w_in.shape[-1] == NUM_GROUPS * d and d % SLAB == 0
    assert ts >= CONV_WIDTH - 1 and ts <= HGRN_CHUNK and tp % 256 == 0
    tm_in = math.gcd(512, math.gcd(n_p, n_s))
    tm_out = math.gcd(256, math.gcd(n_p, n_s))

    lb_all = jnp.cumsum(jax.nn.softmax(lb_logits.astype(F32), axis=0), axis=0)
    lb_all = lb_all - lb_all[0:1]

    xp = x_prompt.reshape(n_p, d)
    xs = x_sample.reshape(n_s, d)
    hist = jnp.pad(cache_conv, ((0, 0), (0, 0), (HIST_ROWS - (CONV_WIDTH - 1), 0), (0, 0)))

    row = lambda a, l: a[l].reshape(1, d)
    h = _prenorm(xp, xs, row(norm_pre_g, 0), tm_in)
    conv_p, hg_p, conv_s, hg_s = [], [], [], []
    for l in range(depth):
        u, szc, q, lf, k, v, szh, sgc, sgh = _inproj(
            h, w_in[l].astype(BF16), lb_all[l].reshape(1, d), tm_in)

        conv_args = (conv_w[l], row(conv_b, l), row(conv_ln_g, l), row(conv_ln_b, l),
                     w_pc[l].astype(BF16))
        mc_p = _conv_long(u, szc, sgc, *conv_args, n_seq=bp, seq_len=tp)
        mc_s = _conv_short(u, szc, sgc, hist[l], *conv_args, n_seq=bs, seq_len=ts, row0=n_p)
        g_h = row(hg_norm_g, l)
        oz_p, sf_p = _hgrn_long(q, k, v, lf, szh, g_h, n_seq=bp, seq_len=tp, name="hgrn_prompt")
        oz_s, sf_s = _hgrn_branch(q, k, v, lf, szh, g_h, state_hgrn[l], n_seq=bs, seq_len=ts,
                                  row0=n_p, name="hgrn_sample")

        last = l == depth - 1
        outs = _out_proj(mc_p, mc_s, oz_p, oz_s, sgh, xp, xs,
                         w_ph[l].astype(BF16), w_out[l].astype(BF16),
                         row(norm_post_g, l), row(norm_pre_g, min(l + 1, depth - 1)),
                         tm=tm_out, emit_h=not last)
        if last:
            xp, xs = outs
        else:
            xp, xs, h = outs

        nb = CONV_WIDTH - 1
        conv_p.append(jnp.stack(
            [lax.slice(u, ((b + 1) * tp - nb, 0), ((b + 1) * tp, d)) for b in range(bp)]
        ).astype(F32))
        conv_s.append(u[n_p:].reshape(bs, ts, d)[:, ts - nb:].astype(F32))
        hg_p.append(sf_p)
        hg_s.append(sf_s)

    return (xp.reshape(bp, tp, d), xs.reshape(bs, ts, d),
            jnp.stack(conv_p), jnp.stack(hg_p), jnp.stack(conv_s), jnp.stack(hg_s))
```

```python
import functools
import math

import numpy as np
import jax
import jax.numpy as jnp
from jax import lax
from jax.experimental import pallas as pl
from jax.experimental.pallas import tpu as pltpu

EPS = 1e-6
HEAD_DIM = 128
CONV_WIDTH = 31
HIST_ROWS = 32
NUM_GROUPS = 9
SLAB = 256
HGRN_CHUNK = 64
LOG_F_PARTS = 2
V7X_VMEM_LIMIT = 56 * 1024 * 1024

F32 = jnp.float32
BF16 = jnp.bfloat16
_NT = (((1,), (1,)), ((), ()))
_TN = (((0,), (0,)), ((), ()))


def _params(n_axes, flags=None):
    return pltpu.CompilerParams(
        dimension_semantics=("arbitrary",) * n_axes,
        vmem_limit_bytes=V7X_VMEM_LIMIT,
        flags=flags,
    )


def _rms(x, g):
    ms = jnp.mean(x * x, axis=-1, keepdims=True)
    return x * lax.rsqrt(ms + EPS) * g


def _prenorm_kernel(n_prompt_blocks, xp_ref, xs_ref, g_ref, h_ref):
    i = pl.program_id(0)

    def body(x_ref):
        h_ref[...] = _rms(x_ref[...], g_ref[...]).astype(BF16)

    pl.when(i < n_prompt_blocks)(lambda: body(xp_ref))
    pl.when(i >= n_prompt_blocks)(lambda: body(xs_ref))


def _prenorm(xp, xs, g, tm):
    n_p, d = xp.shape
    n_s = xs.shape[0]
    npb, nsb = n_p // tm, n_s // tm
    return pl.pallas_call(
        functools.partial(_prenorm_kernel, npb),
        grid=(npb + nsb,),
        in_specs=[
            pl.BlockSpec((tm, d), lambda i: (jnp.minimum(i, npb - 1), 0)),
            pl.BlockSpec((tm, d), lambda i: (jnp.maximum(i - npb, 0), 0)),
            pl.BlockSpec((1, d), lambda i: (0, 0)),
        ],
        out_specs=pl.BlockSpec((tm, d), lambda i: (i, 0)),
        out_shape=jax.ShapeDtypeStruct((n_p + n_s, d), BF16),
        compiler_params=_params(1),
        name="prenorm",
    )(xp, xs, g)


_CONV_FIRST = HIST_ROWS - (CONV_WIDTH - 1)


def _conv_rows(ubuf, cw_ref, r0, rb, cols):
    acc = None
    for res in range(8):
        ext = rb + (8 if res else 0)
        grp = None
        for j in range(CONV_WIDTH):
            off = _CONV_FIRST + j
            if off % 8 != res:
                continue
            base = r0 + off - res
            term = cw_ref[j:j + 1, cols] * ubuf[base:base + ext, cols]
            grp = term if grp is None else grp + term
        part = grp[res:res + rb] if res else grp
        acc = part if acc is None else acc + part
    return acc


def _inproj_kernel(h_ref, *refs):
    w_refs = refs[:NUM_GROUPS]
    (lb_ref, u_ref, szc_ref, q_ref, lf_ref, k_ref, v_ref, szh_ref, sgc_ref,
     sgh_ref) = refs[NUM_GROUPS:]
    h = h_ref[...]

    def mm(g):
        return jnp.dot(h, w_refs[g][...], preferred_element_type=F32)

    u_ref[...] = (mm(0) * jax.nn.sigmoid(mm(1))).astype(BF16)
    zc = mm(2)
    szc_ref[...] = (zc * jax.nn.sigmoid(zc)).astype(BF16)
    q_ref[...] = mm(3).astype(BF16)

    fa = mm(4)
    lb = lb_ref[...]
    softplus_neg = jnp.log(1.0 + jnp.exp(-jnp.abs(fa)))
    log_sig = jnp.minimum(fa, 0.0) - softplus_neg
    log_sig_neg = -jnp.maximum(fa, 0.0) - softplus_neg
    la = jnp.log(lb)
    y = jnp.log1p(-lb) + log_sig
    lf_ref[...] = jnp.maximum(la, y) + jnp.log(1.0 + jnp.exp(-jnp.abs(la - y)))
    k_ref[...] = ((1.0 - lb) * jnp.exp(log_sig_neg)).astype(BF16)

    v_ref[...] = mm(5).astype(BF16)
    zh = mm(6)
    szh_ref[...] = (zh * jax.nn.sigmoid(zh)).astype(BF16)
    sgc_ref[...] = jax.nn.sigmoid(mm(7)).astype(BF16)
    sgh_ref[...] = jax.nn.sigmoid(mm(8)).astype(BF16)


def _inproj(h, w, lb, tm):
    n, d = h.shape
    n_slabs = d // SLAB
    out_spec = pl.BlockSpec((tm, SLAB), lambda j, i: (i, j))
    w_specs = [pl.BlockSpec((d, SLAB), functools.partial(lambda j, i, g: (0, g * n_slabs + j), g=g))
               for g in range(NUM_GROUPS)]
    bf = jax.ShapeDtypeStruct((n, d), BF16)
    f32 = jax.ShapeDtypeStruct((n, d), F32)
    return pl.pallas_call(
        _inproj_kernel,
        grid=(n_slabs, n // tm),
        in_specs=[pl.BlockSpec((tm, d), lambda j, i: (i, 0))] + w_specs
        + [pl.BlockSpec((1, SLAB), lambda j, i: (0, j))],
        out_specs=[out_spec] * 9,
        out_shape=[bf, bf, bf, f32, bf, bf, bf, bf, bf],
        compiler_params=_params(2),
        name="inproj",
    )(h, *([w] * NUM_GROUPS), lb)


def _conv_tail(c, szc, sgc, lng_ref, lnb_ref, wpc_ref):
    mu = jnp.mean(c, axis=-1, keepdims=True)
    cc = c - mu
    var = jnp.mean(cc * cc, axis=-1, keepdims=True)
    y = cc * lax.rsqrt(var + EPS) * lng_ref[...] + lnb_ref[...]
    a = (y * jax.nn.sigmoid(y) * szc.astype(F32)).astype(BF16)
    yc = jnp.dot(a, wpc_ref[...], preferred_element_type=F32)
    return (sgc.astype(F32) * yc).astype(BF16)


def _conv_long_kernel(tt, rb, u_ref, szc_ref, sgc_ref, cw_ref, cb_ref, lng_ref, lnb_ref,
                      wpc_ref, mc_ref, ubuf, cbuf):
    d = u_ref.shape[-1]

    @pl.when(pl.program_id(1) == 0)
    def _():
        ubuf[0:HIST_ROWS, :] = jnp.zeros((HIST_ROWS, d), F32)

    ubuf[HIST_ROWS:HIST_ROWS + tt, :] = u_ref[...].astype(F32)

    def col_body(ci, carry):
        cols = pl.ds(pl.multiple_of(ci * 128, 128), 128)
        for r0 in range(0, tt, rb):
            cbuf[r0:r0 + rb, cols] = _conv_rows(ubuf, cw_ref, r0, rb, cols) + cb_ref[:, cols]
        return carry

    lax.fori_loop(0, d // 128, col_body, 0)
    mc_ref[...] = _conv_tail(cbuf[...], szc_ref[...], sgc_ref[...], lng_ref, lnb_ref, wpc_ref)
    ubuf[0:HIST_ROWS, :] = ubuf[tt:tt + HIST_ROWS, :]


def _conv_long(u, szc, sgc, cw, cb, lng, lnb, wpc, *, n_seq, seq_len):
    d = u.shape[-1]
    tt = 256
    nt = seq_len // tt
    tok = pl.BlockSpec((tt, d), lambda b, t: (b * nt + t, 0))
    vec = pl.BlockSpec((1, d), lambda b, t: (0, 0))
    return pl.pallas_call(
        functools.partial(_conv_long_kernel, tt, 64),
        grid=(n_seq, nt),
        in_specs=[tok, tok, tok, pl.BlockSpec((CONV_WIDTH, d), lambda b, t: (0, 0)),
                  vec, vec, vec,
                  pl.BlockSpec((d, d), lambda b, t: (0, 0), pipeline_mode=pl.Buffered(1))],
        out_specs=tok,
        out_shape=jax.ShapeDtypeStruct((n_seq * seq_len, d), BF16),
        scratch_shapes=[pltpu.VMEM((tt + HIST_ROWS, d), F32), pltpu.VMEM((tt, d), F32)],
        compiler_params=_params(2),
        name="conv_prompt",
    )(u, szc, sgc, cw, cb, lng, lnb, wpc)


def _conv_short_kernel(n_seg, seg, u_ref, szc_ref, sgc_ref, hist_ref, cw_ref, cb_ref,
                       lng_ref, lnb_ref, wpc_ref, mc_ref, ubuf, cbuf):
    d = u_ref.shape[-1]
    stride = seg + HIST_ROWS
    for s in range(n_seg):
        ubuf[s * stride:s * stride + HIST_ROWS, :] = hist_ref[s]
        ubuf[s * stride + HIST_ROWS:(s + 1) * stride, :] = (
            u_ref[s * seg:(s + 1) * seg, :].astype(F32))

    def col_body(ci, carry):
        cols = pl.ds(pl.multiple_of(ci * 128, 128), 128)
        for s in range(n_seg):
            cbuf[s * seg:(s + 1) * seg, cols] = (
                _conv_rows(ubuf, cw_ref, s * stride, seg, cols) + cb_ref[:, cols])
        return carry

    lax.fori_loop(0, d // 128, col_body, 0)
    mc_ref[...] = _conv_tail(cbuf[...], szc_ref[...], sgc_ref[...], lng_ref, lnb_ref, wpc_ref)


def _conv_short(u, szc, sgc, hist, cw, cb, lng, lnb, wpc, *, n_seq, seq_len, row0):
    d = u.shape[-1]
    n_seg = math.gcd(n_seq, max(1, 256 // seq_len))
    tt = n_seg * seq_len
    rb0 = row0 // tt
    tok = pl.BlockSpec((tt, d), lambda b: (rb0 + b, 0))
    vec = pl.BlockSpec((1, d), lambda b: (0, 0))
    return pl.pallas_call(
        functools.partial(_conv_short_kernel, n_seg, seq_len),
        grid=(n_seq // n_seg,),
        in_specs=[tok, tok, tok,
                  pl.BlockSpec((n_seg, HIST_ROWS, d), lambda b: (b, 0, 0)),
                  pl.BlockSpec((CONV_WIDTH, d), lambda b: (0, 0)), vec, vec, vec,
                  pl.BlockSpec((d, d), lambda b: (0, 0), pipeline_mode=pl.Buffered(1))],
        out_specs=pl.BlockSpec((tt, d), lambda b: (b, 0)),
        out_shape=jax.ShapeDtypeStruct((n_seq * seq_len, d), BF16),
        scratch_shapes=[pltpu.VMEM((n_seg * (seq_len + HIST_ROWS), d), F32),
                        pltpu.VMEM((tt, d), F32)],
        compiler_params=_params(1),
        name="conv_sample",
    )(u, szc, sgc, hist, cw, cb, lng, lnb, wpc)


def _hgrn_tables(blk):
    n_levels = int(math.log2(blk))
    r = np.arange(blk)[:, None]
    j = np.arange(blk)[None, :]
    mats = [j <= r, j > r]
    masks = [r == j]
    for lvl in range(n_levels):
        c = blk >> (lvl + 1)
        second = ((r // c) % 2) == 1
        ref = (r // (2 * c)) * 2 * c + c - 1
        mats.append(np.where(second, (j > ref) & (j <= r), (j > r) & (j <= ref)))
        first_s = ((j // c) % 2) == 0
        masks.append(((r // (2 * c)) == (j // (2 * c))) & second & first_s)
    m_all = np.concatenate(mats, axis=0).astype(np.float32)
    kpad = -(-LOG_F_PARTS * blk // 128) * 128
    m3 = np.zeros((m_all.shape[0], kpad), np.float32)
    for s in range(LOG_F_PARTS):
        m3[:, s * blk:(s + 1) * blk] = m_all
    return n_levels, m3, np.stack(masks).astype(np.float32)


def _decays(lf, m3_ref):
    blk, lanes = lf.shape
    kpad = m3_ref.shape[1]
    parts, rest = [], lf
    for _ in range(LOG_F_PARTS):
        part = rest.astype(BF16)
        parts.append(part)
        rest = rest - part.astype(F32)
    if kpad > LOG_F_PARTS * blk:
        parts.append(jnp.zeros((kpad - LOG_F_PARTS * blk, lanes), BF16))
    return jnp.exp(jnp.dot(m3_ref[...], jnp.concatenate(parts, axis=0),
                           preferred_element_type=F32))


def _hgrn_kernel(blk, n_blocks, n_levels, has_s0, *refs):
    if has_s0:
        (q_ref, k_ref, v_ref, lf_ref, szh_ref, g_ref, m3_ref, mask_ref, s0_ref,
         oz_ref, sfin_ref, st_ref) = refs
    else:
        (q_ref, k_ref, v_ref, lf_ref, szh_ref, g_ref, m3_ref, mask_ref,
         oz_ref, sfin_ref, st_ref) = refs
        s0_ref = None
    n_heads = st_ref.shape[0]
    t = pl.program_id(2)

    @pl.when(t == 0)
    def _():
        for hh in range(n_heads):
            if has_s0:
                st_ref[hh] = s0_ref[0, hh].T
            else:
                st_ref[hh] = jnp.zeros((HEAD_DIM, HEAD_DIM), F32)

    def block(bi, carry):
        rows = pl.ds(pl.multiple_of(bi * blk, blk), blk)
        e32 = _decays(lf_ref[rows, :], m3_ref)
        e16 = e32.astype(BF16)
        for hh in range(n_heads):
            ls = slice(hh * HEAD_DIM, (hh + 1) * HEAD_DIM)
            q = q_ref[rows, ls]
            k = k_ref[rows, ls]
            v = v_ref[rows, ls]
            st = st_ref[hh]
            o = lax.dot_general(q * e16[0:blk, ls], st.astype(BF16), _NT,
                                preferred_element_type=F32)
            sc = lax.dot_general(q, k, _NT, preferred_element_type=F32) * mask_ref[0]
            for lvl in range(n_levels):
                ec = e16[(2 + lvl) * blk:(3 + lvl) * blk, ls]
                sc = sc + lax.dot_general(q * ec, k * ec, _NT,
                                          preferred_element_type=F32) * mask_ref[1 + lvl]
            o = o + jnp.dot(sc.astype(BF16), v, preferred_element_type=F32)
            ku = k * e16[blk:2 * blk, ls]
            st_ref[hh] = st * e32[blk - 1:blk, ls] + lax.dot_general(
                v, ku, _TN, preferred_element_type=F32)
            on = _rms(o, g_ref[:, ls])
            oz_ref[rows, ls] = (on * szh_ref[rows, ls].astype(F32)).astype(BF16)
        return carry

    lax.fori_loop(0, n_blocks, block, 0)

    @pl.when(t == pl.num_programs(2) - 1)
    def _():
        for hh in range(n_heads):
            sfin_ref[0, hh] = st_ref[hh].T


def _hgrn_branch(q, k, v, lf, szh, g, s0, *, n_seq, seq_len, row0, name):
    d = q.shape[-1]
    n_heads_total = d // HEAD_DIM
    gw = math.gcd(d, 8 * HEAD_DIM)
    hpg = gw // HEAD_DIM
    blk = min(HGRN_CHUNK, seq_len)
    tt = min(seq_len, 256)
    nt = seq_len // tt
    n_levels, m3, masks = _hgrn_tables(blk)
    rb0 = row0 // tt
    tok = pl.BlockSpec((tt, gw), lambda b, g_, t: (rb0 + b * nt + t, g_))
    state = pl.BlockSpec((1, hpg, HEAD_DIM, HEAD_DIM), lambda b, g_, t: (b, g_, 0, 0))
    in_specs = [tok, tok, tok, tok, tok,
                pl.BlockSpec((1, gw), lambda b, g_, t: (0, g_)),
                pl.BlockSpec(m3.shape, lambda b, g_, t: (0, 0)),
                pl.BlockSpec(masks.shape, lambda b, g_, t: (0, 0, 0))]
    args = [q, k, v, lf, szh, g, jnp.asarray(m3, BF16), jnp.asarray(masks, F32)]
    if s0 is not None:
        in_specs.append(state)
        args.append(s0)
    return pl.pallas_call(
        functools.partial(_hgrn_kernel, blk, tt // blk, n_levels, s0 is not None),
        grid=(n_seq, d // gw, nt),
        in_specs=in_specs,
        out_specs=[pl.BlockSpec((tt, gw), lambda b, g_, t: (b * nt + t, g_)), state],
        out_shape=[jax.ShapeDtypeStruct((n_seq * seq_len, d), BF16),
                   jax.ShapeDtypeStruct((n_seq, n_heads_total, HEAD_DIM, HEAD_DIM), F32)],
        scratch_shapes=[pltpu.VMEM((hpg, HEAD_DIM, HEAD_DIM), F32)],
        compiler_params=_params(3),
        name=name,
    )(*args)


def _hgrn_tile_tables(blk, sub):
    n_fine, m3, _ = _hgrn_tables(blk)
    m3 = m3[:-blk]
    r = np.arange(sub)[:, None]
    j = np.arange(sub)[None, :]
    masks = [r == j]
    c = sub // 2
    while c >= 1:
        second = ((r // c) % 2) == 1
        first_s = ((j // c) % 2) == 0
        masks.append(((r // (2 * c)) == (j // (2 * c))) & second & first_s)
        c //= 2
    return n_fine, m3, np.stack(masks).astype(np.float32)


def _hgrn_tile_kernel(blk, sub, n_pairs, n_fine, q_ref, k_ref, v_ref, lf_ref, szh_ref, g_ref,
                      m3_ref, mask_ref, oz_ref, sfin_ref, st_ref):
    n_heads = st_ref.shape[0]
    t = pl.program_id(2)

    @pl.when(t == 0)
    def _():
        st_ref[...] = jnp.zeros(st_ref.shape, F32)

    def subtile(r0):
        e_a = _decays(lf_ref[pl.ds(r0, blk), :], m3_ref)
        e_b = _decays(lf_ref[pl.ds(pl.multiple_of(r0 + blk, blk), blk), :], m3_ref)
        rows = pl.ds(r0, sub)
        for hh in range(n_heads):
            ls = slice(hh * HEAD_DIM, (hh + 1) * HEAD_DIM)
            q = q_ref[rows, ls]
            k = k_ref[rows, ls]
            v = v_ref[rows, ls]
            p_a, r_a = e_a[0:blk, ls], e_a[blk:2 * blk, ls]
            p_b, r_b = e_b[0:blk, ls], e_b[blk:2 * blk, ls]
            d_a, d_b = p_a[blk - 1:blk, :], p_b[blk - 1:blk, :]

            q32, k32 = q.astype(F32), k.astype(F32)
            f_t = jnp.exp(lf_ref[rows, ls])
            sc = jnp.sum(q32 * k32, axis=-1, keepdims=True) * mask_ref[0]
            sc = sc + jnp.sum(q32 * f_t * pltpu.roll(k32, 1, 0), axis=-1,
                              keepdims=True) * mask_ref[1 + n_fine]
            e_x = jnp.concatenate([r_a, p_b], axis=0).astype(BF16)
            sc = sc + lax.dot_general(q * e_x, k * e_x, _NT,
                                      preferred_element_type=F32) * mask_ref[1]
            for lvl in range(n_fine - 1):
                sl = slice((2 + lvl) * blk, (3 + lvl) * blk)
                ec = jnp.concatenate([e_a[sl, ls], e_b[sl, ls]], axis=0).astype(BF16)
                sc = sc + lax.dot_general(q * ec, k * ec, _NT,
                                          preferred_element_type=F32) * mask_ref[2 + lvl]

            e_in = jnp.concatenate([p_a, p_b * d_a], axis=0).astype(BF16)
            e_up = jnp.concatenate([r_a * d_b, r_b], axis=0).astype(BF16)
            st = st_ref[hh]
            o = lax.dot_general(q * e_in, st.astype(BF16), _NT, preferred_element_type=F32)
            o = o + jnp.dot(sc.astype(BF16), v, preferred_element_type=F32)
            st_ref[hh] = st * (d_a * d_b) + lax.dot_general(
                v, k * e_up, _TN, preferred_element_type=F32)
            on = _rms(o, g_ref[:, ls])
            oz_ref[rows, ls] = (on * szh_ref[rows, ls].astype(F32)).astype(BF16)

    def pair(pi, carry):
        r0 = pl.multiple_of(pi * (2 * sub), 2 * sub)
        subtile(r0)
        subtile(pl.multiple_of(r0 + sub, sub))
        return carry

    lax.fori_loop(0, n_pairs, pair, 0)

    @pl.when(t == pl.num_programs(2) - 1)
    def _():
        for hh in range(n_heads):
            sfin_ref[0, hh] = st_ref[hh].T


def _hgrn_long(q, k, v, lf, szh, g, *, n_seq, seq_len, name):
    d = q.shape[-1]
    gw = math.gcd(d, 4 * HEAD_DIM)
    hpg = gw // HEAD_DIM
    blk, sub = HGRN_CHUNK, 2 * HGRN_CHUNK
    tt = 512 if seq_len % 512 == 0 else 2 * sub
    nt = seq_len // tt
    n_fine, m3, masks = _hgrn_tile_tables(blk, sub)
    tok = pl.BlockSpec((tt, gw), lambda b, g_, t: (b * nt + t, g_))
    state = pl.BlockSpec((1, hpg, HEAD_DIM, HEAD_DIM), lambda b, g_, t: (b, g_, 0, 0))
    return pl.pallas_call(
        functools.partial(_hgrn_tile_kernel, blk, sub, tt // (2 * sub), n_fine),
        grid=(n_seq, d // gw, nt),
        in_specs=[tok, tok, tok, tok, tok,
                  pl.BlockSpec((1, gw), lambda b, g_, t: (0, g_)),
                  pl.BlockSpec(m3.shape, lambda b, g_, t: (0, 0)),
                  pl.BlockSpec(masks.shape, lambda b, g_, t: (0, 0, 0))],
        out_specs=[tok, state],
        out_shape=[jax.ShapeDtypeStruct((n_seq * seq_len, d), BF16),
                   jax.ShapeDtypeStruct((n_seq, d // HEAD_DIM, HEAD_DIM, HEAD_DIM), F32)],
        scratch_shapes=[pltpu.VMEM((hpg, HEAD_DIM, HEAD_DIM), F32)],
        compiler_params=_params(3),
        name=name,
    )(q, k, v, lf, szh, g, jnp.asarray(m3, BF16), jnp.asarray(masks, F32))


def _out_kernel(n_prompt_blocks, emit_h, *refs):
    (mcp_ref, mcs_ref, ozp_ref, ozs_ref, sgh_ref, xp_ref, xs_ref,
     wph_ref, wout_ref, gpost_ref, gnext_ref) = refs[:11]
    if emit_h:
        yp_ref, ys_ref, hn_ref = refs[11:]
    else:
        yp_ref, ys_ref = refs[11:]
        hn_ref = None
    i = pl.program_id(0)

    def body(mc_ref, oz_ref, x_ref, y_ref):
        yh = jnp.dot(oz_ref[...], wph_ref[...], preferred_element_type=F32)
        m = mc_ref[...].astype(F32) + sgh_ref[...].astype(F32) * yh
        o = jnp.dot(m.astype(BF16), wout_ref[...], preferred_element_type=F32)
        y = x_ref[...] + _rms(o, gpost_ref[...])
        y_ref[...] = y
        if emit_h:
            hn_ref[...] = _rms(y, gnext_ref[...]).astype(BF16)

    pl.when(i < n_prompt_blocks)(lambda: body(mcp_ref, ozp_ref, xp_ref, yp_ref))
    pl.when(i >= n_prompt_blocks)(lambda: body(mcs_ref, ozs_ref, xs_ref, ys_ref))


def _out_proj(mc_p, mc_s, oz_p, oz_s, sgh, xp, xs, wph, wout, gpost, gnext,
              *, tm, emit_h):
    n_p, d = xp.shape
    n_s = xs.shape[0]
    npb, nsb = n_p // tm, n_s // tm
    pr = pl.BlockSpec((tm, d), lambda i: (jnp.minimum(i, npb - 1), 0))
    sa = pl.BlockSpec((tm, d), lambda i: (jnp.maximum(i - npb, 0), 0))
    al = pl.BlockSpec((tm, d), lambda i: (i, 0))
    wt = pl.BlockSpec((d, d), lambda i: (0, 0), pipeline_mode=pl.Buffered(1))
    vec = pl.BlockSpec((1, d), lambda i: (0, 0))
    out_specs = [pr, sa]
    out_shape = [jax.ShapeDtypeStruct((n_p, d), F32), jax.ShapeDtypeStruct((n_s, d), F32)]
    if emit_h:
        out_specs.append(al)
        out_shape.append(jax.ShapeDtypeStruct((n_p + n_s, d), BF16))
    return pl.pallas_call(
        functools.partial(_out_kernel, npb, emit_h),
        grid=(npb + nsb,),
        in_specs=[pr, sa, pr, sa, al, pr, sa, wt, wt, vec, vec],
        out_specs=out_specs,
        out_shape=out_shape,
        compiler_params=_params(1),
        name="outproj",
    )(mc_p, mc_s, oz_p, oz_s, sgh, xp, xs, wph, wout, gpost, gnext)


def kernel(x_prompt, x_sample, cache_conv, state_hgrn, lb_logits, w_in, conv_w, conv_b,
           conv_ln_g, conv_ln_b, hg_norm_g, w_pc, w_ph, w_out, norm_pre_g, norm_post_g):
    bp, tp, d = x_prompt.shape
    bs, ts, _ = x_sample.shape
    depth = w_in.shape[0]
    n_p, n_s = bp * tp, bs * ts
    assert w_in.shape[-1] == NUM_GROUPS * d and d % SLAB == 0
    assert ts >= CONV_WIDTH - 1 and ts <= HGRN_CHUNK and tp % 256 == 0
    tm_in = math.gcd(1024, math.gcd(n_p, n_s))
    tm_out = math.gcd(256, math.gcd(n_p, n_s))

    lb_all = jnp.cumsum(jax.nn.softmax(lb_logits.astype(F32), axis=0), axis=0)
    lb_all = lb_all - lb_all[0:1]

    xp = x_prompt.reshape(n_p, d)
    xs = x_sample.reshape(n_s, d)
    hist = jnp.pad(cache_conv, ((0, 0), (0, 0), (HIST_ROWS - (CONV_WIDTH - 1), 0), (0, 0)))

    row = lambda a, l: a[l].reshape(1, d)
    h = _prenorm(xp, xs, row(norm_pre_g, 0), tm_in)
    conv_p, hg_p, conv_s, hg_s = [], [], [], []
    for l in range(depth):
        u, szc, q, lf, k, v, szh, sgc, sgh = _inproj(
            h, w_in[l].astype(BF16), lb_all[l].reshape(1, d), tm_in)

        conv_args = (conv_w[l], row(conv_b, l), row(conv_ln_g, l), row(conv_ln_b, l),
                     w_pc[l].astype(BF16))
        mc_p = _conv_long(u, szc, sgc, *conv_args, n_seq=bp, seq_len=tp)
        mc_s = _conv_short(u, szc, sgc, hist[l], *conv_args, n_seq=bs, seq_len=ts, row0=n_p)
        g_h = row(hg_norm_g, l)
        oz_p, sf_p = _hgrn_long(q, k, v, lf, szh, g_h, n_seq=bp, seq_len=tp, name="hgrn_prompt")
        oz_s, sf_s = _hgrn_branch(q, k, v, lf, szh, g_h, state_hgrn[l], n_seq=bs, seq_len=ts,
                                  row0=n_p, name="hgrn_sample")

        last = l == depth - 1
        outs = _out_proj(mc_p, mc_s, oz_p, oz_s, sgh, xp, xs,
                         w_ph[l].astype(BF16), w_out[l].astype(BF16),
                         row(norm_post_g, l), row(norm_pre_g, min(l + 1, depth - 1)),
                         tm=tm_out, emit_h=not last)
        if last:
            xp, xs = outs
        else:
            xp, xs, h = outs

        nb = CONV_WIDTH - 1
        conv_p.append(jnp.stack(
            [lax.slice(u, ((b + 1) * tp - nb, 0), ((b + 1) * tp, d)) for b in range(bp)]
        ).astype(F32))
        conv_s.append(u[n_p:].reshape(bs, ts, d)[:, ts - nb:].astype(F32))
        hg_p.append(sf_p)
        hg_s.append(sf_s)

    return (xp.reshape(bp, tp, d), xs.reshape(bs, ts, d),
            jnp.stack(conv_p), jnp.stack(hg_p), jnp.stack(conv_s), jnp.stack(hg_s))
```

```python
import functools
import math

import numpy as np
import jax
import jax.numpy as jnp
from jax import lax
from jax.experimental import pallas as pl
from jax.experimental.pallas import tpu as pltpu

EPS = 1e-6
HEAD_DIM = 128
CONV_WIDTH = 31
HIST_ROWS = 32
CONV_ROWS = 64
NUM_GROUPS = 9
SLAB = 256
HGRN_CHUNK = 64
LOG_F_PARTS = 2
V7X_VMEM_LIMIT = 56 * 1024 * 1024

F32 = jnp.float32
BF16 = jnp.bfloat16
_NT = (((1,), (1,)), ((), ()))
_TN = (((0,), (0,)), ((), ()))


def _params(n_axes, flags=None):
    return pltpu.CompilerParams(
        dimension_semantics=("arbitrary",) * n_axes,
        vmem_limit_bytes=V7X_VMEM_LIMIT,
        flags=flags,
    )


def _rms(x, g):
    ms = jnp.mean(x * x, axis=-1, keepdims=True)
    return x * lax.rsqrt(ms + EPS) * g


def _prenorm_kernel(n_prompt_blocks, xp_ref, xs_ref, g_ref, h_ref):
    i = pl.program_id(0)

    def body(x_ref):
        h_ref[...] = _rms(x_ref[...], g_ref[...]).astype(BF16)

    pl.when(i < n_prompt_blocks)(lambda: body(xp_ref))
    pl.when(i >= n_prompt_blocks)(lambda: body(xs_ref))


def _prenorm(xp, xs, g, tm):
    n_p, d = xp.shape
    n_s = xs.shape[0]
    npb, nsb = n_p // tm, n_s // tm
    return pl.pallas_call(
        functools.partial(_prenorm_kernel, npb),
        grid=(npb + nsb,),
        in_specs=[
            pl.BlockSpec((tm, d), lambda i: (jnp.minimum(i, npb - 1), 0)),
            pl.BlockSpec((tm, d), lambda i: (jnp.maximum(i - npb, 0), 0)),
            pl.BlockSpec((1, d), lambda i: (0, 0)),
        ],
        out_specs=pl.BlockSpec((tm, d), lambda i: (i, 0)),
        out_shape=jax.ShapeDtypeStruct((n_p + n_s, d), BF16),
        compiler_params=_params(1),
        name="prenorm",
    )(xp, xs, g)


_CONV_FIRST = HIST_ROWS - (CONV_WIDTH - 1)


def _conv_rows(ubuf, cw_ref, r0, rb, cols):
    acc = None
    for res in range(8):
        ext = rb + (8 if res else 0)
        grp = None
        for j in range(CONV_WIDTH):
            off = _CONV_FIRST + j
            if off % 8 != res:
                continue
            base = r0 + off - res
            term = cw_ref[j:j + 1, cols] * ubuf[base:base + ext, cols]
            grp = term if grp is None else grp + term
        part = grp[res:res + rb] if res else grp
        acc = part if acc is None else acc + part
    return acc


def _inproj_kernel(h_ref, *refs):
    w_refs = refs[:NUM_GROUPS]
    (lb_ref, u_ref, szc_ref, q_ref, lf_ref, k_ref, v_ref, szh_ref, sgc_ref,
     sgh_ref) = refs[NUM_GROUPS:]
    h = h_ref[...]

    def mm(g):
        return jnp.dot(h, w_refs[g][...], preferred_element_type=F32)

    u_ref[...] = (mm(0) * jax.nn.sigmoid(mm(1))).astype(BF16)
    zc = mm(2)
    szc_ref[...] = (zc * jax.nn.sigmoid(zc)).astype(BF16)
    q_ref[...] = mm(3).astype(BF16)

    fa = mm(4)
    lb = lb_ref[...]
    softplus_neg = jnp.log(1.0 + jnp.exp(-jnp.abs(fa)))
    log_sig = jnp.minimum(fa, 0.0) - softplus_neg
    log_sig_neg = -jnp.maximum(fa, 0.0) - softplus_neg
    la = jnp.log(lb)
    y = jnp.log1p(-lb) + log_sig
    lf_ref[...] = jnp.maximum(la, y) + jnp.log(1.0 + jnp.exp(-jnp.abs(la - y)))
    k_ref[...] = ((1.0 - lb) * jnp.exp(log_sig_neg)).astype(BF16)

    v_ref[...] = mm(5).astype(BF16)
    zh = mm(6)
    szh_ref[...] = (zh * jax.nn.sigmoid(zh)).astype(BF16)
    sgc_ref[...] = jax.nn.sigmoid(mm(7)).astype(BF16)
    sgh_ref[...] = jax.nn.sigmoid(mm(8)).astype(BF16)


def _inproj(h, w, layer, lb, tm):
    n, d = h.shape
    n_slabs = d // SLAB
    out_spec = pl.BlockSpec((tm, SLAB), lambda j, i: (i, j))
    w_specs = [pl.BlockSpec((None, d, SLAB),
                            functools.partial(lambda j, i, g: (layer, 0, g * n_slabs + j), g=g))
               for g in range(NUM_GROUPS)]
    bf = jax.ShapeDtypeStruct((n, d), BF16)
    f32 = jax.ShapeDtypeStruct((n, d), F32)
    return pl.pallas_call(
        _inproj_kernel,
        grid=(n_slabs, n // tm),
        in_specs=[pl.BlockSpec((tm, d), lambda j, i: (i, 0))] + w_specs
        + [pl.BlockSpec((1, SLAB), lambda j, i: (0, j))],
        out_specs=[out_spec] * 9,
        out_shape=[bf, bf, bf, f32, bf, bf, bf, bf, bf],
        compiler_params=_params(2),
        name="inproj",
    )(h, *([w] * NUM_GROUPS), lb)


def _conv_tail(c, szc, sgc, lng_ref, lnb_ref, wpc_ref):
    mu = jnp.mean(c, axis=-1, keepdims=True)
    cc = c - mu
    var = jnp.mean(cc * cc, axis=-1, keepdims=True)
    y = cc * lax.rsqrt(var + EPS) * lng_ref[...] + lnb_ref[...]
    a = (y * jax.nn.sigmoid(y) * szc.astype(F32)).astype(BF16)
    yc = jnp.dot(a, wpc_ref[...], preferred_element_type=F32)
    return (sgc.astype(F32) * yc).astype(BF16)


def _convtail_kernel(c_ref, szc_ref, sgc_ref, lng_ref, lnb_ref, wpc_ref, mc_ref):
    mc_ref[...] = _conv_tail(c_ref[...].astype(F32), szc_ref[...], sgc_ref[...],
                             lng_ref, lnb_ref, wpc_ref)


def _convtail(c, szc, sgc, lng, lnb, wpc, layer, *, n_rows, tm):
    d = c.shape[-1]
    tok = pl.BlockSpec((tm, d), lambda i: (i, 0))
    vec = pl.BlockSpec((1, d), lambda i: (0, 0))
    return pl.pallas_call(
        _convtail_kernel,
        grid=(n_rows // tm,),
        in_specs=[tok, tok, tok, vec, vec,
                  pl.BlockSpec((None, d, d), lambda i: (layer, 0, 0),
                               pipeline_mode=pl.Buffered(1))],
        out_specs=tok,
        out_shape=jax.ShapeDtypeStruct((n_rows, d), BF16),
        compiler_params=_params(1),
        name="convtail",
    )(c, szc, sgc, lng, lnb, wpc)


def _conv_short_kernel(n_seg, seg, u_ref, szc_ref, sgc_ref, hist_ref, cw_ref, cb_ref,
                       lng_ref, lnb_ref, wpc_ref, mc_ref, ubuf, cbuf):
    d = u_ref.shape[-1]
    stride = seg + HIST_ROWS
    for s in range(n_seg):
        ubuf[s * stride:s * stride + HIST_ROWS, :] = hist_ref[s]
        ubuf[s * stride + HIST_ROWS:(s + 1) * stride, :] = (
            u_ref[s * seg:(s + 1) * seg, :].astype(F32))

    def col_body(ci, carry):
        cols = pl.ds(pl.multiple_of(ci * 128, 128), 128)
        for s in range(n_seg):
            cbuf[s * seg:(s + 1) * seg, cols] = (
                _conv_rows(ubuf, cw_ref, s * stride, seg, cols) + cb_ref[:, cols])
        return carry

    lax.fori_loop(0, d // 128, col_body, 0)
    mc_ref[...] = _conv_tail(cbuf[...], szc_ref[...], sgc_ref[...], lng_ref, lnb_ref, wpc_ref)


def _conv_short(u, szc, sgc, hist, cw, cb, lng, lnb, wpc, layer, *, n_seq, seq_len, row0):
    d = u.shape[-1]
    n_seg = math.gcd(n_seq, max(1, 256 // seq_len))
    tt = n_seg * seq_len
    rb0 = row0 // tt
    tok = pl.BlockSpec((tt, d), lambda b: (rb0 + b, 0))
    vec = pl.BlockSpec((1, d), lambda b: (0, 0))
    return pl.pallas_call(
        functools.partial(_conv_short_kernel, n_seg, seq_len),
        grid=(n_seq // n_seg,),
        in_specs=[tok, tok, tok,
                  pl.BlockSpec((None, n_seg, HIST_ROWS, d), lambda b: (layer, b, 0, 0)),
                  pl.BlockSpec((CONV_WIDTH, d), lambda b: (0, 0)), vec, vec, vec,
                  pl.BlockSpec((None, d, d), lambda b: (layer, 0, 0),
                               pipeline_mode=pl.Buffered(1))],
        out_specs=pl.BlockSpec((tt, d), lambda b: (b, 0)),
        out_shape=jax.ShapeDtypeStruct((n_seq * seq_len, d), BF16),
        scratch_shapes=[pltpu.VMEM((n_seg * (seq_len + HIST_ROWS), d), F32),
                        pltpu.VMEM((tt, d), F32)],
        compiler_params=_params(1),
        name="conv_sample",
    )(u, szc, sgc, hist, cw, cb, lng, lnb, wpc)


def _hgrn_tables(blk):
    n_levels = int(math.log2(blk))
    r = np.arange(blk)[:, None]
    j = np.arange(blk)[None, :]
    mats = [j <= r, j > r]
    masks = [r == j]
    for lvl in range(n_levels):
        c = blk >> (lvl + 1)
        second = ((r // c) % 2) == 1
        ref = (r // (2 * c)) * 2 * c + c - 1
        mats.append(np.where(second, (j > ref) & (j <= r), (j > r) & (j <= ref)))
        first_s = ((j // c) % 2) == 0
        masks.append(((r // (2 * c)) == (j // (2 * c))) & second & first_s)
    m_all = np.concatenate(mats, axis=0).astype(np.float32)
    kpad = -(-LOG_F_PARTS * blk // 128) * 128
    m3 = np.zeros((m_all.shape[0], kpad), np.float32)
    for s in range(LOG_F_PARTS):
        m3[:, s * blk:(s + 1) * blk] = m_all
    return n_levels, m3, np.stack(masks).astype(np.float32)


def _decays(lf, m3_ref):
    blk, lanes = lf.shape
    kpad = m3_ref.shape[1]
    parts, rest = [], lf
    for _ in range(LOG_F_PARTS):
        part = rest.astype(BF16)
        parts.append(part)
        rest = rest - part.astype(F32)
    if kpad > LOG_F_PARTS * blk:
        parts.append(jnp.zeros((kpad - LOG_F_PARTS * blk, lanes), BF16))
    return jnp.exp(jnp.dot(m3_ref[...], jnp.concatenate(parts, axis=0),
                           preferred_element_type=F32))


def _hgrn_kernel(blk, n_blocks, n_levels, has_s0, *refs):
    if has_s0:
        (q_ref, k_ref, v_ref, lf_ref, szh_ref, g_ref, m3_ref, mask_ref, s0_ref,
         oz_ref, sfin_ref, st_ref) = refs
    else:
        (q_ref, k_ref, v_ref, lf_ref, szh_ref, g_ref, m3_ref, mask_ref,
         oz_ref, sfin_ref, st_ref) = refs
        s0_ref = None
    n_heads = st_ref.shape[0]
    t = pl.program_id(2)

    @pl.when(t == 0)
    def _():
        for hh in range(n_heads):
            if has_s0:
                st_ref[hh] = s0_ref[0, hh].T
            else:
                st_ref[hh] = jnp.zeros((HEAD_DIM, HEAD_DIM), F32)

    def block(bi, carry):
        rows = pl.ds(pl.multiple_of(bi * blk, blk), blk)
        e32 = _decays(lf_ref[rows, :], m3_ref)
        e16 = e32.astype(BF16)
        for hh in range(n_heads):
            ls = slice(hh * HEAD_DIM, (hh + 1) * HEAD_DIM)
            q = q_ref[rows, ls]
            k = k_ref[rows, ls]
            v = v_ref[rows, ls]
            st = st_ref[hh]
            o = lax.dot_general(q * e16[0:blk, ls], st.astype(BF16), _NT,
                                preferred_element_type=F32)
            sc = lax.dot_general(q, k, _NT, preferred_element_type=F32) * mask_ref[0]
            for lvl in range(n_levels):
                ec = e16[(2 + lvl) * blk:(3 + lvl) * blk, ls]
                sc = sc + lax.dot_general(q * ec, k * ec, _NT,
                                          preferred_element_type=F32) * mask_ref[1 + lvl]
            o = o + jnp.dot(sc.astype(BF16), v, preferred_element_type=F32)
            ku = k * e16[blk:2 * blk, ls]
            st_ref[hh] = st * e32[blk - 1:blk, ls] + lax.dot_general(
                v, ku, _TN, preferred_element_type=F32)
            on = _rms(o, g_ref[:, ls])
            oz_ref[rows, ls] = (on * szh_ref[rows, ls].astype(F32)).astype(BF16)
        return carry

    lax.fori_loop(0, n_blocks, block, 0)

    @pl.when(t == pl.num_programs(2) - 1)
    def _():
        for hh in range(n_heads):
            sfin_ref[0, hh] = st_ref[hh].T


def _hgrn_branch(q, k, v, lf, szh, g, s0, layer, *, n_seq, seq_len, row0, name):
    d = q.shape[-1]
    n_heads_total = d // HEAD_DIM
    gw = math.gcd(d, 8 * HEAD_DIM)
    hpg = gw // HEAD_DIM
    blk = min(HGRN_CHUNK, seq_len)
    tt = min(seq_len, 256)
    nt = seq_len // tt
    n_levels, m3, masks = _hgrn_tables(blk)
    rb0 = row0 // tt
    tok = pl.BlockSpec((tt, gw), lambda b, g_, t: (rb0 + b * nt + t, g_))
    state = pl.BlockSpec((1, hpg, HEAD_DIM, HEAD_DIM), lambda b, g_, t: (b, g_, 0, 0))
    in_specs = [tok, tok, tok, tok, tok,
                pl.BlockSpec((1, gw), lambda b, g_, t: (0, g_)),
                pl.BlockSpec(m3.shape, lambda b, g_, t: (0, 0)),
                pl.BlockSpec(masks.shape, lambda b, g_, t: (0, 0, 0))]
    args = [q, k, v, lf, szh, g, jnp.asarray(m3, BF16), jnp.asarray(masks, F32)]
    if s0 is not None:
        in_specs.append(pl.BlockSpec((None, 1, hpg, HEAD_DIM, HEAD_DIM),
                                     lambda b, g_, t: (layer, b, g_, 0, 0)))
        args.append(s0)
    return pl.pallas_call(
        functools.partial(_hgrn_kernel, blk, tt // blk, n_levels, s0 is not None),
        grid=(n_seq, d // gw, nt),
        in_specs=in_specs,
        out_specs=[pl.BlockSpec((tt, gw), lambda b, g_, t: (b * nt + t, g_)), state],
        out_shape=[jax.ShapeDtypeStruct((n_seq * seq_len, d), BF16),
                   jax.ShapeDtypeStruct((n_seq, n_heads_total, HEAD_DIM, HEAD_DIM), F32)],
        scratch_shapes=[pltpu.VMEM((hpg, HEAD_DIM, HEAD_DIM), F32)],
        compiler_params=_params(3),
        name=name,
    )(*args)


def _hgrn_tile_tables(blk, sub):
    n_fine, m3, _ = _hgrn_tables(blk)
    m3 = m3[:-blk]
    r = np.arange(sub)[:, None]
    j = np.arange(sub)[None, :]
    masks = [r == j]
    c = sub // 2
    while c >= 1:
        second = ((r // c) % 2) == 1
        first_s = ((j // c) % 2) == 0
        masks.append(((r // (2 * c)) == (j // (2 * c))) & second & first_s)
        c //= 2
    return n_fine, m3, np.stack(masks).astype(np.float32)


def _hgrn_tile_kernel(blk, sub, n_pairs, n_fine, q_ref, k_ref, v_ref, lf_ref, szh_ref, g_ref,
                      m3_ref, mask_ref, u_ref, cw_ref, cb_ref, oz_ref, sfin_ref, c_ref,
                      st_ref, ubuf):
    n_heads = st_ref.shape[0]
    tt, gw = u_ref.shape
    t = pl.program_id(2)

    @pl.when(t == 0)
    def _():
        st_ref[...] = jnp.zeros(st_ref.shape, F32)
        ubuf[0:HIST_ROWS, :] = jnp.zeros((HIST_ROWS, gw), F32)

    ubuf[HIST_ROWS:HIST_ROWS + tt, :] = u_ref[...].astype(F32)

    def subtile(r0):
        e_a = _decays(lf_ref[pl.ds(r0, blk), :], m3_ref)
        e_b = _decays(lf_ref[pl.ds(pl.multiple_of(r0 + blk, blk), blk), :], m3_ref)
        rows = pl.ds(r0, sub)
        for hh in range(n_heads):
            ls = slice(hh * HEAD_DIM, (hh + 1) * HEAD_DIM)
            q = q_ref[rows, ls]
            k = k_ref[rows, ls]
            v = v_ref[rows, ls]
            p_a, r_a = e_a[0:blk, ls], e_a[blk:2 * blk, ls]
            p_b, r_b = e_b[0:blk, ls], e_b[blk:2 * blk, ls]
            d_a, d_b = p_a[blk - 1:blk, :], p_b[blk - 1:blk, :]

            q32, k32 = q.astype(F32), k.astype(F32)
            f_t = jnp.exp(lf_ref[rows, ls])
            sc = jnp.sum(q32 * k32, axis=-1, keepdims=True) * mask_ref[0]
            sc = sc + jnp.sum(q32 * f_t * pltpu.roll(k32, 1, 0), axis=-1,
                              keepdims=True) * mask_ref[1 + n_fine]
            e_x = jnp.concatenate([r_a, p_b], axis=0).astype(BF16)
            sc = sc + lax.dot_general(q * e_x, k * e_x, _NT,
                                      preferred_element_type=F32) * mask_ref[1]
            for lvl in range(n_fine - 1):
                sl = slice((2 + lvl) * blk, (3 + lvl) * blk)
                ec = jnp.concatenate([e_a[sl, ls], e_b[sl, ls]], axis=0).astype(BF16)
                sc = sc + lax.dot_general(q * ec, k * ec, _NT,
                                          preferred_element_type=F32) * mask_ref[2 + lvl]

            e_in =jnp.concatenate([p_a, p_b * d_a], axis=0).astype(BF16)
            e_up = jnp.concatenate([r_a * d_b, r_b], axis=0).astype(BF16)
            st = st_ref[hh]
            o = lax.dot_general(q * e_in, st.astype(BF16), _NT, preferred_element_type=F32)
            o = o + jnp.dot(sc.astype(BF16), v, preferred_element_type=F32)
            st_ref[hh] = st * (d_a * d_b) + lax.dot_general(
                v, k * e_up, _TN, preferred_element_type=F32)
            on = _rms(o, g_ref[:, ls])
            oz_ref[rows, ls] = (on * szh_ref[rows, ls].astype(F32)).astype(BF16)

    for s in range(2 * n_pairs):
        subtile(s * sub)

    for r0 in range(0, tt, CONV_ROWS):
        for c0 in range(0, gw, 128):
            cols = slice(c0, c0 + 128)
            c_ref[r0:r0 + CONV_ROWS, cols] = (
                _conv_rows(ubuf, cw_ref, r0, CONV_ROWS, cols) + cb_ref[:, cols]).astype(BF16)

    ubuf[0:HIST_ROWS, :] = ubuf[tt:tt + HIST_ROWS, :]

    @pl.when(t == pl.num_programs(2) - 1)
    def _():
        for hh in range(n_heads):
            sfin_ref[0, hh] = st_ref[hh].T


def _hgrn_long(q, k, v, lf, szh, g, u, cw, cb, *, n_seq, seq_len, name):
    d = q.shape[-1]
    gw = math.gcd(d, 4 * HEAD_DIM)
    hpg = gw // HEAD_DIM
    blk, sub = HGRN_CHUNK, 2 * HGRN_CHUNK
    tt = 512 if seq_len % 512 == 0 else 2 * sub
    nt = seq_len // tt
    n_fine, m3, masks = _hgrn_tile_tables(blk, sub)
    tok = pl.BlockSpec((tt, gw), lambda b, g_, t: (b * nt + t, g_))
    state = pl.BlockSpec((1, hpg, HEAD_DIM, HEAD_DIM), lambda b, g_, t: (b, g_, 0, 0))
    return pl.pallas_call(
        functools.partial(_hgrn_tile_kernel, blk, sub, tt // (2 * sub), n_fine),
        grid=(n_seq, d // gw, nt),
        in_specs=[tok, tok, tok, tok, tok,
                  pl.BlockSpec((1, gw), lambda b, g_, t: (0, g_)),
                  pl.BlockSpec(m3.shape, lambda b, g_, t: (0, 0)),
                  pl.BlockSpec(masks.shape, lambda b, g_, t: (0, 0, 0)),
                  tok,
                  pl.BlockSpec((CONV_WIDTH, gw), lambda b, g_, t: (0, g_)),
                  pl.BlockSpec((1, gw), lambda b, g_, t: (0, g_))],
        out_specs=[tok, state, tok],
        out_shape=[jax.ShapeDtypeStruct((n_seq * seq_len, d), BF16),
                   jax.ShapeDtypeStruct((n_seq, d // HEAD_DIM, HEAD_DIM, HEAD_DIM), F32),
                   jax.ShapeDtypeStruct((n_seq * seq_len, d), BF16)],
        scratch_shapes=[pltpu.VMEM((hpg, HEAD_DIM, HEAD_DIM), F32),
                        pltpu.VMEM((HIST_ROWS + tt, gw), F32)],
        compiler_params=_params(3),
        name=name,
    )(q, k, v, lf, szh, g, jnp.asarray(m3, BF16), jnp.asarray(masks, F32), u, cw, cb)


def _out_kernel(n_prompt_blocks, emit_h, *refs):
    (mcp_ref, mcs_ref, ozp_ref, ozs_ref, sgh_ref, xp_ref, xs_ref,
     wph_ref, wout_ref, gpost_ref, gnext_ref) = refs[:11]
    if emit_h:
        yp_ref, ys_ref, hn_ref = refs[11:]
    else:
        yp_ref, ys_ref = refs[11:]
        hn_ref = None
    i = pl.program_id(0)

    def body(mc_ref, oz_ref, x_ref, y_ref):
        yh = jnp.dot(oz_ref[...], wph_ref[...], preferred_element_type=F32)
        m = mc_ref[...].astype(F32) + sgh_ref[...].astype(F32) * yh
        o = jnp.dot(m.astype(BF16), wout_ref[...], preferred_element_type=F32)
        y = x_ref[...] + _rms(o, gpost_ref[...])
        y_ref[...] = y
        if emit_h:
            hn_ref[...] = _rms(y, gnext_ref[...]).astype(BF16)

    pl.when(i < n_prompt_blocks)(lambda: body(mcp_ref, ozp_ref, xp_ref, yp_ref))
    pl.when(i >= n_prompt_blocks)(lambda: body(mcs_ref, ozs_ref, xs_ref, ys_ref))


def _out_proj(mc_p, mc_s, oz_p, oz_s, sgh, xp, xs, wph, wout, layer, gpost, gnext,
              *, tm, emit_h):
    n_p, d = xp.shape
    n_s = xs.shape[0]
    npb, nsb = n_p // tm, n_s // tm
    pr = pl.BlockSpec((tm, d), lambda i: (jnp.minimum(i, npb - 1), 0))
    sa = pl.BlockSpec((tm, d), lambda i: (jnp.maximum(i - npb, 0), 0))
    al = pl.BlockSpec((tm, d), lambda i: (i, 0))
    wt = pl.BlockSpec((None, d, d), lambda i: (layer, 0, 0), pipeline_mode=pl.Buffered(1))
    vec = pl.BlockSpec((1, d), lambda i: (0, 0))
    out_specs = [pr, sa]
    out_shape = [jax.ShapeDtypeStruct((n_p, d), F32), jax.ShapeDtypeStruct((n_s, d), F32)]
    if emit_h:
        out_specs.append(al)
        out_shape.append(jax.ShapeDtypeStruct((n_p + n_s, d), BF16))
    return pl.pallas_call(
        functools.partial(_out_kernel, npb, emit_h),
        grid=(npb + nsb,),
        in_specs=[pr, sa, pr, sa, al, pr, sa, wt, wt, vec, vec],
        out_specs=out_specs,
        out_shape=out_shape,
        compiler_params=_params(1),
        name="outproj",
    )(mc_p, mc_s, oz_p, oz_s, sgh, xp, xs, wph, wout, gpost, gnext)


def kernel(x_prompt, x_sample, cache_conv, state_hgrn, lb_logits, w_in, conv_w, conv_b,
           conv_ln_g, conv_ln_b, hg_norm_g, w_pc, w_ph, w_out, norm_pre_g, norm_post_g):
    bp, tp, d = x_prompt.shape
    bs, ts, _ = x_sample.shape
    depth = w_in.shape[0]
    n_p, n_s = bp * tp, bs * ts
    assert w_in.shape[-1] == NUM_GROUPS * d and d % SLAB == 0
    assert ts >= CONV_WIDTH - 1 and ts <= HGRN_CHUNK and tp % 256 == 0
    tm_in = math.gcd(1024, math.gcd(n_p, n_s))
    tm_out = math.gcd(256, math.gcd(n_p, n_s))
    tm_tail = math.gcd(512, n_p)

    lb_all = jnp.cumsum(jax.nn.softmax(lb_logits.astype(F32), axis=0), axis=0)
    lb_all = lb_all - lb_all[0:1]

    xp = x_prompt.reshape(n_p, d)
    xs = x_sample.reshape(n_s, d)
    hist = jnp.pad(cache_conv, ((0, 0), (0, 0), (HIST_ROWS - (CONV_WIDTH - 1), 0), (0, 0)))

    w_in16, w_pc16, w_ph16, w_out16 = (w.astype(BF16) for w in (w_in, w_pc, w_ph, w_out))
    row = lambda a, l: a[l].reshape(1, d)
    h = _prenorm(xp, xs, row(norm_pre_g, 0), tm_in)
    conv_p, hg_p, conv_s, hg_s = [], [], [], []
    for l in range(depth):
        u, szc, q, lf, k, v, szh, sgc, sgh = _inproj(
            h, w_in16, l, lb_all[l].reshape(1, d), tm_in)

        g_h = row(hg_norm_g, l)
        oz_p, sf_p, c_p = _hgrn_long(q, k, v, lf, szh, g_h, u, conv_w[l], row(conv_b, l),
                                     n_seq=bp, seq_len=tp, name="hgrn_conv_prompt")
        ln_args = (row(conv_ln_g, l), row(conv_ln_b, l), w_pc16, l)
        mc_p = _convtail(c_p, szc, sgc, *ln_args, n_rows=n_p, tm=tm_tail)
        mc_s = _conv_short(u, szc, sgc, hist, conv_w[l], row(conv_b, l), *ln_args,
                           n_seq=bs, seq_len=ts, row0=n_p)
        oz_s, sf_s = _hgrn_branch(q, k, v, lf, szh, g_h, state_hgrn, l, n_seq=bs, seq_len=ts,
                                  row0=n_p, name="hgrn_sample")

        last = l == depth - 1
        outs = _out_proj(mc_p, mc_s, oz_p, oz_s, sgh, xp, xs, w_ph16, w_out16, l,
                         row(norm_post_g, l), row(norm_pre_g, min(l + 1, depth - 1)),
                         tm=tm_out, emit_h=not last)
        if last:
            xp, xs = outs
        else:
            xp, xs, h = outs

        nb = CONV_WIDTH - 1
        conv_p.append(jnp.stack(
            [lax.slice(u, ((b + 1) * tp - nb, 0), ((b + 1) * tp, d)) for b in range(bp)]
        ).astype(F32))
        conv_s.append(u[n_p:].reshape(bs, ts, d)[:, ts - nb:].astype(F32))
        hg_p.append(sf_p)
        hg_s.append(sf_s)

    return (xp.reshape(bp, tp, d), xs.reshape(bs, ts, d),
            jnp.stack(conv_p), jnp.stack(hg_p), jnp.stack(conv_s), jnp.stack(hg_s))
```

```python
import functools
import math

import numpy as np
import jax
import jax.numpy as jnp
from jax import lax
from jax.experimental import pallas as pl
from jax.experimental.pallas import tpu as pltpu

EPS = 1e-6
HEAD_DIM = 128
CONV_WIDTH = 31
HIST_ROWS = 32
CONV_ROWS = 64
NUM_GROUPS = 9
SLAB = 256
HGRN_CHUNK = 64
LOG_F_PARTS = 2
TM_IN = 1024
TM_OUT = 256
TM_TAIL = 512
HGRN_LONG_ROWS = 512
HGRN_LONG_HEADS = 4
HGRN_SHORT_HEADS = 8
CONV_SHORT_ROWS = 256
V7X_VMEM_LIMIT = 56 * 1024 * 1024

F32 = jnp.float32
BF16 = jnp.bfloat16
_NT = (((1,), (1,)), ((), ()))
_TN = (((0,), (0,)), ((), ()))


def _params(n_axes, flags=None):
    return pltpu.CompilerParams(
        dimension_semantics=("arbitrary",) * n_axes,
        vmem_limit_bytes=V7X_VMEM_LIMIT,
        flags=flags,
    )


def _rms(x, g):
    ms = jnp.mean(x * x, axis=-1, keepdims=True)
    return x * lax.rsqrt(ms + EPS) * g


def _prenorm_kernel(n_prompt_blocks, xp_ref, xs_ref, g_ref, h_ref):
    i = pl.program_id(0)

    def body(x_ref):
        h_ref[...] = _rms(x_ref[...], g_ref[...]).astype(BF16)

    pl.when(i < n_prompt_blocks)(lambda: body(xp_ref))
    pl.when(i >= n_prompt_blocks)(lambda: body(xs_ref))


def _prenorm(xp, xs, g, tm):
    n_p, d = xp.shape
    n_s = xs.shape[0]
    npb, nsb = n_p // tm, n_s // tm
    return pl.pallas_call(
        functools.partial(_prenorm_kernel, npb),
        grid=(npb + nsb,),
        in_specs=[
            pl.BlockSpec((tm, d), lambda i: (jnp.minimum(i, npb - 1), 0)),
            pl.BlockSpec((tm, d), lambda i: (jnp.maximum(i - npb, 0), 0)),
            pl.BlockSpec((1, d), lambda i: (0, 0)),
        ],
        out_specs=pl.BlockSpec((tm, d), lambda i: (i, 0)),
        out_shape=jax.ShapeDtypeStruct((n_p + n_s, d), BF16),
        compiler_params=_params(1),
        name="prenorm",
    )(xp, xs, g)


def _inproj_kernel(h_ref, *refs):
    w_refs = refs[:NUM_GROUPS]
    (lb_ref, u_ref, szc_ref, q_ref, lf_ref, v_ref, szh_ref, sgc_ref,
     sgh_ref) = refs[NUM_GROUPS:]
    h = h_ref[...]

    def mm(g):
        return jnp.dot(h, w_refs[g][...], preferred_element_type=F32)

    u_ref[...] = (mm(0) * jax.nn.sigmoid(mm(1))).astype(BF16)
    zc = mm(2)
    szc_ref[...] = (zc * jax.nn.sigmoid(zc)).astype(BF16)
    q_ref[...] = mm(3).astype(BF16)

    fa = mm(4)
    lb = lb_ref[...]
    log_sig = jnp.minimum(fa, 0.0) - jnp.log(1.0 + jnp.exp(-jnp.abs(fa)))
    la = jnp.log(lb)
    y = jnp.log1p(-lb) + log_sig
    lf_ref[...] = jnp.maximum(la, y) + jnp.log(1.0 + jnp.exp(-jnp.abs(la - y)))

    v_ref[...] = mm(5).astype(BF16)
    zh = mm(6)
    szh_ref[...] = (zh * jax.nn.sigmoid(zh)).astype(BF16)
    sgc_ref[...] = jax.nn.sigmoid(mm(7)).astype(BF16)
    sgh_ref[...] = jax.nn.sigmoid(mm(8)).astype(BF16)


def _inproj(h, w, layer, lb, tm):
    n, d = h.shape
    n_slabs = d // SLAB
    out_spec = pl.BlockSpec((tm, SLAB), lambda j, i: (i, j))
    w_specs = [pl.BlockSpec((None, d, SLAB),
                            functools.partial(lambda j, i, g: (layer, 0, g * n_slabs + j), g=g))
               for g in range(NUM_GROUPS)]
    bf = jax.ShapeDtypeStruct((n, d), BF16)
    f32 = jax.ShapeDtypeStruct((n, d), F32)
    return pl.pallas_call(
        _inproj_kernel,
        grid=(n_slabs, n // tm),
        in_specs=[pl.BlockSpec((tm, d), lambda j, i: (i, 0))] + w_specs
        + [pl.BlockSpec((1, SLAB), lambda j, i: (0, j))],
        out_specs=[out_spec] * 8,
        out_shape=[bf, bf, bf, f32, bf, bf, bf, bf],
        compiler_params=_params(2),
        name="inproj",
    )(h, *([w] * NUM_GROUPS), lb)


_CONV_FIRST = HIST_ROWS - (CONV_WIDTH - 1)
SUBLANES = 8


def _conv_rows(ubuf, cw_ref, r0, rb, cols):
    acc = None
    for res in range(SUBLANES):
        ext = rb + (SUBLANES if res else 0)
        grp = None
        for j in range(CONV_WIDTH):
            off = _CONV_FIRST + j
            if off % SUBLANES != res:
                continue
            base = r0 + off - res
            term = cw_ref[j:j + 1, cols] * ubuf[base:base + ext, cols]
            grp = term if grp is None else grp + term
        part = grp[res:res + rb] if res else grp
        acc = part if acc is None else acc + part
    return acc


def _conv_tail(c, szc, sgc, lng_ref, lnb_ref, wpc_ref):
    mu = jnp.mean(c, axis=-1, keepdims=True)
    cc = c - mu
    var = jnp.mean(cc * cc, axis=-1, keepdims=True)
    y = cc * lax.rsqrt(var + EPS) * lng_ref[...] + lnb_ref[...]
    a = (y * jax.nn.sigmoid(y) * szc.astype(F32)).astype(BF16)
    yc = jnp.dot(a, wpc_ref[...], preferred_element_type=F32)
    return (sgc.astype(F32) * yc).astype(BF16)


def _convtail_kernel(c_ref, szc_ref, sgc_ref, lng_ref, lnb_ref, wpc_ref, mc_ref):
    mc_ref[...] = _conv_tail(c_ref[...].astype(F32), szc_ref[...], sgc_ref[...],
                             lng_ref, lnb_ref, wpc_ref)


def _convtail(c, szc, sgc, lng, lnb, wpc, layer, *, n_rows, tm):
    d = c.shape[-1]
    tok = pl.BlockSpec((tm, d), lambda i: (i, 0))
    vec = pl.BlockSpec((1, d), lambda i: (0, 0))
    return pl.pallas_call(
        _convtail_kernel,
        grid=(n_rows // tm,),
        in_specs=[tok, tok, tok, vec, vec,
                  pl.BlockSpec((None, d, d), lambda i: (layer, 0, 0),
                               pipeline_mode=pl.Buffered(1))],
        out_specs=tok,
        out_shape=jax.ShapeDtypeStruct((n_rows, d), BF16),
        compiler_params=_params(1),
        name="convtail",
    )(c, szc, sgc, lng, lnb, wpc)


def _conv_short_kernel(n_seg, seg, u_ref, szc_ref, sgc_ref, hist_ref, cw_ref, cb_ref,
                       lng_ref, lnb_ref, wpc_ref, mc_ref, ubuf, cbuf):
    d = u_ref.shape[-1]
    stride = seg + HIST_ROWS
    for s in range(n_seg):
        ubuf[s * stride:s * stride + HIST_ROWS, :] = hist_ref[s]
        ubuf[s * stride + HIST_ROWS:(s + 1) * stride, :] = (
            u_ref[s * seg:(s + 1) * seg, :].astype(F32))

    def col_body(ci, carry):
        cols = pl.ds(pl.multiple_of(ci * 128, 128), 128)
        for s in range(n_seg):
            cbuf[s * seg:(s + 1) * seg, cols] = (
                _conv_rows(ubuf, cw_ref, s * stride, seg, cols) + cb_ref[:, cols])
        return carry

    lax.fori_loop(0, d // 128, col_body, 0)
    mc_ref[...] = _conv_tail(cbuf[...], szc_ref[...], sgc_ref[...], lng_ref, lnb_ref, wpc_ref)


def _conv_short(u, szc, sgc, hist, cw, cb, lng, lnb, wpc, layer, *, n_seq, seq_len, row0):
    d = u.shape[-1]
    n_seg = math.gcd(n_seq, max(1, CONV_SHORT_ROWS // seq_len))
    tt = n_seg * seq_len
    rb0 = row0 // tt
    tok = pl.BlockSpec((tt, d), lambda b: (rb0 + b, 0))
    vec = pl.BlockSpec((1, d), lambda b: (0, 0))
    return pl.pallas_call(
        functools.partial(_conv_short_kernel, n_seg, seq_len),
        grid=(n_seq // n_seg,),
        in_specs=[tok, tok, tok,
                  pl.BlockSpec((None, n_seg, HIST_ROWS, d), lambda b: (layer, b, 0, 0)),
                  pl.BlockSpec((CONV_WIDTH, d), lambda b: (0, 0)), vec, vec, vec,
                  pl.BlockSpec((None, d, d), lambda b: (layer, 0, 0),
                               pipeline_mode=pl.Buffered(1))],
        out_specs=pl.BlockSpec((tt, d), lambda b: (b, 0)),
        out_shape=jax.ShapeDtypeStruct((n_seq * seq_len, d), BF16),
        scratch_shapes=[pltpu.VMEM((n_seg * (seq_len + HIST_ROWS), d), F32),
                        pltpu.VMEM((tt, d), F32)],
        compiler_params=_params(1),
        name="conv_sample",
    )(u, szc, sgc, hist, cw, cb, lng, lnb, wpc)


def _hgrn_tables(blk):
    n_levels = int(math.log2(blk))
    r = np.arange(blk)[:, None]
    j = np.arange(blk)[None, :]
    mats = [j <= r, j > r]
    masks = [r == j]
    for lvl in range(n_levels):
        c = blk >> (lvl + 1)
        second = ((r // c) % 2) == 1
        ref = (r // (2 * c)) * 2 * c + c - 1
        mats.append(np.where(second, (j > ref) & (j <= r), (j > r) & (j <= ref)))
        first_s = ((j // c) % 2) == 0
        masks.append(((r // (2 * c)) == (j // (2 * c))) & second & first_s)
    m_all = np.concatenate(mats, axis=0).astype(np.float32)
    kpad = -(-LOG_F_PARTS * blk // 128) * 128
    m3 = np.zeros((m_all.shape[0], kpad), np.float32)
    for s in range(LOG_F_PARTS):
        m3[:, s * blk:(s + 1) * blk] = m_all
    return n_levels, m3, np.stack(masks).astype(np.float32)


def _decays(lf, m3_ref):
    blk, lanes = lf.shape
    kpad = m3_ref.shape[1]
    parts, rest = [], lf
    for _ in range(LOG_F_PARTS):
        part = rest.astype(BF16)
        parts.append(part)
        rest = rest - part.astype(F32)
    if kpad > LOG_F_PARTS * blk:
        parts.append(jnp.zeros((kpad - LOG_F_PARTS * blk, lanes), BF16))
    return jnp.exp(jnp.dot(m3_ref[...], jnp.concatenate(parts, axis=0),
                           preferred_element_type=F32))


def _hgrn_short_kernel(n_levels, q_ref, v_ref, lf_ref, szh_ref, g_ref, m3_ref, mask_ref,
                       s0_ref, oz_ref, sfin_ref):
    blk = q_ref.shape[0]
    n_heads = s0_ref.shape[1]
    e32 = _decays(lf_ref[...], m3_ref)
    e16 = e32.astype(BF16)
    for hh in range(n_heads):
        ls = slice(hh * HEAD_DIM, (hh + 1) * HEAD_DIM)
        q = q_ref[:, ls]
        v = v_ref[:, ls]
        k = (1.0 - jnp.exp(lf_ref[:, ls])).astype(BF16)
        st = s0_ref[0, hh].T
        o = lax.dot_general(q * e16[0:blk, ls], st.astype(BF16), _NT,
                            preferred_element_type=F32)
        sc = lax.dot_general(q, k, _NT, preferred_element_type=F32) * mask_ref[0]
        for lvl in range(n_levels):
            ec = e16[(2 + lvl) * blk:(3 + lvl) * blk, ls]
            sc = sc + lax.dot_general(q * ec, k * ec, _NT,
                                      preferred_element_type=F32) * mask_ref[1 + lvl]
        o = o + jnp.dot(sc.astype(BF16), v, preferred_element_type=F32)
        ku = k * e16[blk:2 * blk, ls]
        st = st * e32[blk - 1:blk, ls] + lax.dot_general(v, ku, _TN, preferred_element_type=F32)
        sfin_ref[0, hh] = st.T
        on = _rms(o, g_ref[:, ls])
        oz_ref[:, ls] = (on * szh_ref[:, ls].astype(F32)).astype(BF16)


def _hgrn_short(q, v, lf, szh, g, s0, layer, *, n_seq, seq_len, row0):
    d = q.shape[-1]
    gw = math.gcd(d, HGRN_SHORT_HEADS * HEAD_DIM)
    hpg = gw // HEAD_DIM
    n_levels, m3, masks = _hgrn_tables(seq_len)
    rb0 = row0 // seq_len
    tok = pl.BlockSpec((seq_len, gw), lambda b, g_: (rb0 + b, g_))
    return pl.pallas_call(
        functools.partial(_hgrn_short_kernel, n_levels),
        grid=(n_seq, d // gw),
        in_specs=[tok, tok, tok, tok,
                  pl.BlockSpec((1, gw), lambda b, g_: (0, g_)),
                  pl.BlockSpec(m3.shape, lambda b, g_: (0, 0)),
                  pl.BlockSpec(masks.shape, lambda b, g_: (0, 0, 0)),
                  pl.BlockSpec((None, 1, hpg, HEAD_DIM, HEAD_DIM),
                               lambda b, g_: (layer, b, g_, 0, 0))],
        out_specs=[pl.BlockSpec((seq_len, gw), lambda b, g_: (b, g_)),
                   pl.BlockSpec((1, hpg, HEAD_DIM, HEAD_DIM), lambda b, g_: (b, g_, 0, 0))],
        out_shape=[jax.ShapeDtypeStruct((n_seq * seq_len, d), BF16),
                   jax.ShapeDtypeStruct((n_seq, d // HEAD_DIM, HEAD_DIM, HEAD_DIM), F32)],
        compiler_params=_params(2),
        name="hgrn_sample",
    )(q, v, lf, szh, g, jnp.asarray(m3, BF16), jnp.asarray(masks, F32), s0)


def _hgrn_tile_tables(blk, sub):
    n_fine, m3, _ = _hgrn_tables(blk)
    m3 = m3[:-blk]
    r = np.arange(sub)[:, None]
    j = np.arange(sub)[None, :]
    masks = [r == j]
    c = sub // 2
    while c >= 1:
        second = ((r // c) % 2) == 1
        first_s = ((j // c) % 2) == 0
        masks.append(((r // (2 * c)) == (j // (2 * c))) & second & first_s)
        c //= 2
    return n_fine, m3, np.stack(masks).astype(np.float32)


def _hgrn_tile_kernel(blk, sub, n_fine, q_ref, v_ref, lf_ref, szh_ref, g_ref,
                      m3_ref, mask_ref, u_ref, cw_ref, cb_ref, oz_ref, sfin_ref, c_ref,
                      st_ref, ubuf):
    n_heads = st_ref.shape[0]
    tt, gw = u_ref.shape
    t = pl.program_id(2)

    @pl.when(t == 0)
    def _():
        st_ref[...] = jnp.zeros(st_ref.shape, F32)
        ubuf[0:HIST_ROWS, :] = jnp.zeros((HIST_ROWS, gw), F32)

    ubuf[HIST_ROWS:HIST_ROWS + tt, :] = u_ref[...].astype(F32)

    def subtile(r0):
        e_a = _decays(lf_ref[pl.ds(r0, blk), :], m3_ref)
        e_b = _decays(lf_ref[pl.ds(pl.multiple_of(r0 + blk, blk), blk), :], m3_ref)
        rows = pl.ds(r0, sub)
        for hh in range(n_heads):
            ls = slice(hh * HEAD_DIM, (hh + 1) * HEAD_DIM)
            q = q_ref[rows, ls]
            v = v_ref[rows, ls]
            f_t = jnp.exp(lf_ref[rows, ls])
            k32 = 1.0 - f_t
            k = k32.astype(BF16)
            p_a, r_a = e_a[0:blk, ls], e_a[blk:2 * blk, ls]
            p_b, r_b = e_b[0:blk, ls], e_b[blk:2 * blk, ls]
            d_a, d_b = p_a[blk - 1:blk, :], p_b[blk - 1:blk, :]

            q32 = q.astype(F32)
            sc = jnp.sum(q32 * k32, axis=-1, keepdims=True) * mask_ref[0]
            sc = sc + jnp.sum(q32 * f_t * pltpu.roll(k32, 1, 0), axis=-1,
                              keepdims=True) * mask_ref[1 + n_fine]
            e_x = jnp.concatenate([r_a, p_b], axis=0).astype(BF16)
            sc = sc + lax.dot_general(q * e_x, k * e_x, _NT,
                                      preferred_element_type=F32) * mask_ref[1]
            for lvl in range(n_fine - 1):
                sl = slice((2 + lvl) * blk, (3 + lvl) * blk)
                ec = jnp.concatenate([e_a[sl, ls], e_b[sl, ls]], axis=0).astype(BF16)
                sc = sc + lax.dot_general(q * ec, k * ec, _NT,
                                          preferred_element_type=F32) * mask_ref[2 + lvl]

            e_in =jnp.concatenate([p_a, p_b * d_a], axis=0).astype(BF16)
            e_up = jnp.concatenate([r_a * d_b, r_b], axis=0).astype(BF16)
            st = st_ref[hh]
            o = lax.dot_general(q * e_in, st.astype(BF16), _NT, preferred_element_type=F32)
            o = o + jnp.dot(sc.astype(BF16), v, preferred_element_type=F32)
            st_ref[hh] = st * (d_a * d_b) + lax.dot_general(
                v, k * e_up, _TN, preferred_element_type=F32)
            on = _rms(o, g_ref[:, ls])
            oz_ref[rows, ls] = (on * szh_ref[rows, ls].astype(F32)).astype(BF16)

    for r0 in range(0, tt, sub):
        subtile(r0)

    for r0 in range(0, tt, CONV_ROWS):
        for c0 in range(0, gw, 128):
            cols = slice(c0, c0 + 128)
            c_ref[r0:r0 + CONV_ROWS, cols] = (
                _conv_rows(ubuf, cw_ref, r0, CONV_ROWS, cols) + cb_ref[:, cols]).astype(BF16)

    ubuf[0:HIST_ROWS, :] = ubuf[tt:tt + HIST_ROWS, :]

    @pl.when(t == pl.num_programs(2) - 1)
    def _():
        for hh in range(n_heads):
            sfin_ref[0, hh] = st_ref[hh].T


def _hgrn_conv_long(q, v, lf, szh, g, u, cw, cb, *, n_seq, seq_len):
    d = q.shape[-1]
    gw = math.gcd(d, HGRN_LONG_HEADS * HEAD_DIM)
    hpg = gw // HEAD_DIM
    blk, sub = HGRN_CHUNK, 2 * HGRN_CHUNK
    tt = math.gcd(seq_len, HGRN_LONG_ROWS)
    nt = seq_len // tt
    n_fine, m3, masks = _hgrn_tile_tables(blk, sub)
    tok = pl.BlockSpec((tt, gw), lambda b, g_, t: (b * nt + t, g_))
    state = pl.BlockSpec((1, hpg, HEAD_DIM, HEAD_DIM), lambda b, g_, t: (b, g_, 0, 0))
    return pl.pallas_call(
        functools.partial(_hgrn_tile_kernel, blk, sub, n_fine),
        grid=(n_seq, d // gw, nt),
        in_specs=[tok, tok, tok, tok,
                  pl.BlockSpec((1, gw), lambda b, g_, t: (0, g_)),
                  pl.BlockSpec(m3.shape, lambda b, g_, t: (0, 0)),
                  pl.BlockSpec(masks.shape, lambda b, g_, t: (0, 0, 0)),
                  tok,
                  pl.BlockSpec((CONV_WIDTH, gw), lambda b, g_, t: (0, g_)),
                  pl.BlockSpec((1, gw), lambda b, g_, t: (0, g_))],
        out_specs=[tok, state, tok],
        out_shape=[jax.ShapeDtypeStruct((n_seq * seq_len, d), BF16),
                   jax.ShapeDtypeStruct((n_seq, d // HEAD_DIM, HEAD_DIM, HEAD_DIM), F32),
                   jax.ShapeDtypeStruct((n_seq * seq_len, d), BF16)],
        scratch_shapes=[pltpu.VMEM((hpg, HEAD_DIM, HEAD_DIM), F32),
                        pltpu.VMEM((HIST_ROWS + tt, gw), F32)],
        compiler_params=_params(3),
        name="hgrn_conv_prompt",
    )(q, v, lf, szh, g, jnp.asarray(m3, BF16), jnp.asarray(masks, F32), u, cw, cb)


def _out_kernel(n_prompt_blocks, emit_h, *refs):
    (mcp_ref, mcs_ref, ozp_ref, ozs_ref, sgh_ref, xp_ref, xs_ref,
     wph_ref, wout_ref, gpost_ref, gnext_ref) = refs[:11]
    if emit_h:
        yp_ref, ys_ref, hn_ref = refs[11:]
    else:
        yp_ref, ys_ref = refs[11:]
        hn_ref = None
    i = pl.program_id(0)

    def body(mc_ref, oz_ref, x_ref, y_ref):
        yh = jnp.dot(oz_ref[...], wph_ref[...], preferred_element_type=F32)
        m = mc_ref[...].astype(F32) + sgh_ref[...].astype(F32) * yh
        o = jnp.dot(m.astype(BF16), wout_ref[...], preferred_element_type=F32)
        y = x_ref[...] + _rms(o, gpost_ref[...])
        y_ref[...] = y
        if emit_h:
            hn_ref[...] = _rms(y, gnext_ref[...]).astype(BF16)

    pl.when(i < n_prompt_blocks)(lambda: body(mcp_ref, ozp_ref, xp_ref, yp_ref))
    pl.when(i >= n_prompt_blocks)(lambda: body(mcs_ref, ozs_ref, xs_ref, ys_ref))


def _out_proj(mc_p, mc_s, oz_p, oz_s, sgh, xp, xs, wph, wout, layer, gpost, gnext,
              *, tm, emit_h):
    n_p, d = xp.shape
    n_s = xs.shape[0]
    npb, nsb = n_p // tm, n_s // tm
    pr = pl.BlockSpec((tm, d), lambda i: (jnp.minimum(i, npb - 1), 0))
    sa = pl.BlockSpec((tm, d), lambda i: (jnp.maximum(i - npb, 0), 0))
    al = pl.BlockSpec((tm, d), lambda i: (i, 0))
    wt = pl.BlockSpec((None, d, d), lambda i: (layer, 0, 0), pipeline_mode=pl.Buffered(1))
    vec = pl.BlockSpec((1, d), lambda i: (0, 0))
    out_specs = [pr, sa]
    out_shape = [jax.ShapeDtypeStruct((n_p, d), F32), jax.ShapeDtypeStruct((n_s, d), F32)]
    if emit_h:
        out_specs.append(al)
        out_shape.append(jax.ShapeDtypeStruct((n_p + n_s, d), BF16))
    return pl.pallas_call(
        functools.partial(_out_kernel, npb, emit_h),
        grid=(npb + nsb,),
        in_specs=[pr, sa, pr, sa, al, pr, sa, wt, wt, vec, vec],
        out_specs=out_specs,
        out_shape=out_shape,
        compiler_params=_params(1),
        name="outproj",
    )(mc_p, mc_s, oz_p, oz_s, sgh, xp, xs, wph, wout, gpost, gnext)


def kernel(x_prompt, x_sample, cache_conv, state_hgrn, lb_logits, w_in, conv_w, conv_b,
           conv_ln_g, conv_ln_b, hg_norm_g, w_pc, w_ph, w_out, norm_pre_g, norm_post_g):
    bp, tp, d = x_prompt.shape
    bs, ts, _ = x_sample.shape
    depth = w_in.shape[0]
    n_p, n_s = bp * tp, bs * ts
    assert w_in.shape[-1] == NUM_GROUPS * d and d % SLAB == 0
    assert ts >= CONV_WIDTH - 1 and ts <= HGRN_CHUNK and tp % 256 == 0
    tm_in = math.gcd(TM_IN, math.gcd(n_p, n_s))
    tm_out = math.gcd(TM_OUT, math.gcd(n_p, n_s))
    tm_tail = math.gcd(TM_TAIL, n_p)

    lb_all = jnp.cumsum(jax.nn.softmax(lb_logits.astype(F32), axis=0), axis=0)
    lb_all = lb_all - lb_all[0:1]

    xp = x_prompt.reshape(n_p, d)
    xs = x_sample.reshape(n_s, d)
    hist = jnp.pad(cache_conv, ((0, 0), (0, 0), (HIST_ROWS - (CONV_WIDTH - 1), 0), (0, 0)))

    w_in16, w_pc16, w_ph16, w_out16 = (w.astype(BF16) for w in (w_in, w_pc, w_ph, w_out))
    row = lambda a, l: a[l].reshape(1, d)
    h = _prenorm(xp, xs, row(norm_pre_g, 0), tm_in)
    conv_p, hg_p, conv_s, hg_s = [], [], [], []
    for l in range(depth):
        u, szc, q, lf, v, szh, sgc, sgh = _inproj(h, w_in16, l, lb_all[l].reshape(1, d), tm_in)

        g_h = row(hg_norm_g, l)
        oz_p, sf_p, c_p = _hgrn_conv_long(q, v, lf, szh, g_h, u, conv_w[l], row(conv_b, l),
                                          n_seq=bp, seq_len=tp)
        ln_args = (row(conv_ln_g, l), row(conv_ln_b, l), w_pc16, l)
        mc_p = _convtail(c_p, szc, sgc, *ln_args, n_rows=n_p, tm=tm_tail)
        mc_s = _conv_short(u, szc, sgc, hist, conv_w[l], row(conv_b, l), *ln_args,
                           n_seq=bs, seq_len=ts, row0=n_p)
        oz_s, sf_s = _hgrn_short(q, v, lf, szh, g_h, state_hgrn, l, n_seq=bs, seq_len=ts,
                                 row0=n_p)

        last = l == depth - 1
        outs = _out_proj(mc_p, mc_s, oz_p, oz_s, sgh, xp, xs, w_ph16, w_out16, l,
                         row(norm_post_g, l), row(norm_pre_g, min(l + 1, depth - 1)),
                         tm=tm_out, emit_h=not last)
        if last:
            xp, xs = outs
        else:
            xp, xs, h = outs

        nb = CONV_WIDTH - 1
        conv_p.append(jnp.stack(
            [lax.slice(u, ((b + 1) * tp - nb, 0), ((b + 1) * tp, d)) for b in range(bp)]
        ).astype(F32))
        conv_s.append(u[n_p:].reshape(bs, ts, d)[:, ts - nb:].astype(F32))
        hg_p.append(sf_p)
        hg_s.append(sf_s)

    return (xp.reshape(bp, tp, d), xs.reshape(bs, ts, d),
            jnp.stack(conv_p), jnp.stack(hg_p), jnp.stack(conv_s), jnp.stack(hg_s))
```
